```python
import math
import jax, jax.numpy as jnp
from jax import lax
import numpy as np

D_MODEL = 2048
BATCH = 2
SEQ = 16384
DEPTH = 2

CHUNK = 64
N_META = 16
Q_BLOCK = 128
ROPE_THETA = 10000.0
NORM_EPS = 1e-6

DA_HEADS = 4
DA_HEAD_DIM = 128
DA_V_DIM = 2 * DA_HEAD_DIM
DA_QK_W = DA_HEADS * 2 * DA_HEAD_DIM
DA_V_W = DA_HEADS * DA_V_DIM

GDN_HEADS = 8
GDN_HEAD_DIM = 128
GDN_W = GDN_HEADS * GDN_HEAD_DIM
GDN_CONV = 4

RW_HEAD_DIM = 64
RW_HEADS = 16
RW_W = RW_HEADS * RW_HEAD_DIM
RW_DECAY_LORA = 64
RW_A_LORA = 64
RW_GATE_LORA = 160
RW_GN_EPS = RW_HEAD_DIM * 1e-5

N_BRANCH = 3
D_FF = 4 * D_MODEL

DA_COLS = 2 * DA_QK_W + DA_V_W
GDN_COLS = 4 * GDN_W + 2 * GDN_HEADS
RW_COLS = 3 * RW_W + RW_DECAY_LORA + RW_A_LORA + RW_GATE_LORA
GATE_COLS = N_BRANCH * D_MODEL
IN_COLS = DA_COLS + GDN_COLS + RW_COLS + GATE_COLS

kernel_name = 'hybrid_diffattn_gdn_rwkv7_trunk'


def rms_norm(x, gain, eps=NORM_EPS):
    xf = x.astype(jnp.float32)
    y = xf * lax.rsqrt(jnp.mean(xf * xf, axis=-1, keepdims=True) + eps)
    return (y * gain.astype(jnp.float32)).astype(x.dtype)


def l2norm(x, eps=1e-6):
    return x * lax.rsqrt(jnp.sum(x * x, axis=-1, keepdims=True) + eps)


def rope_tables(n_pos, dim):
    inv = 1.0 / (ROPE_THETA ** (jnp.arange(0, dim, 2, dtype=jnp.float32) / dim))
    ang = jnp.arange(n_pos, dtype=jnp.float32)[:, None] * inv[None, :]
    return jnp.cos(ang), jnp.sin(ang)


def apply_rope(x, cos, sin):
    half = x.shape[-1] // 2
    shape = (x.shape[1],) + (1,) * (x.ndim - 3) + (half,)
    c, s = cos.reshape(shape), sin.reshape(shape)
    x1, x2 = x[..., :half], x[..., half:]
    return jnp.concatenate([x1 * c - x2 * s, x1 * s + x2 * c], axis=-1).astype(x.dtype)


def chunk_index(l):
    real = 1 + jnp.arange(l - N_META, dtype=jnp.int32) // CHUNK
    return jnp.concatenate([jnp.zeros((N_META,), jnp.int32), real])


def causal_dwconv(x, w):
    k = w.shape[0]
    return lax.conv_general_dilated(x, w[:, None, :].astype(x.dtype), window_strides=(1,),
                                    padding=[(k - 1, 0)], dimension_numbers=('NWC', 'WIO', 'NWC'),
                                    feature_group_count=x.shape[-1])


def token_shift(z):
    return jnp.pad(z, ((0, 0), (1, 0), (0, 0)))[:, :-1]


def diff_attention(q, k, v, lam, cos, sin):
    b, l = q.shape[0], q.shape[1]
    n_real = l - N_META
    q = apply_rope(q, cos, sin) * (DA_HEAD_DIM ** -0.5)
    k = apply_rope(k, cos, sin)
    key_chunk = chunk_index(l)
    s_m = jnp.einsum('bqhmd,bkhmd->bhmqk', q[:, :N_META], k[:, :N_META], preferred_element_type=jnp.float32)
    p_m = jax.nn.softmax(s_m, axis=-1)
    o_m = jnp.einsum('bhqk,bkhe->bqhe', (p_m[:, :, 0] - lam * p_m[:, :, 1]).astype(v.dtype), v[:, :N_META])
    n_blk = n_real // Q_BLOCK
    q_blocks = jnp.moveaxis(q[:, N_META:].reshape(b, n_blk, Q_BLOCK, DA_HEADS, 2, DA_HEAD_DIM), 1, 0)
    q_chunk = key_chunk[N_META:].reshape(n_blk, Q_BLOCK)

    def one_block(args):
        qb, qc = args
        s = jnp.einsum('bqhmd,bkhmd->bhmqk', qb, k, preferred_element_type=jnp.float32)
        visible = key_chunk[None, :] <= qc[:, None]
        p = jax.nn.softmax(jnp.where(visible, s, -jnp.inf), axis=-1)
        pd = (p[:, :, 0] - lam * p[:, :, 1]).astype(v.dtype)
        return jnp.einsum('bhqk,bkhe->bqhe', pd, v)

    o_r = lax.map(one_block, (q_blocks, q_chunk))
    o_r = jnp.moveaxis(o_r, 0, 1).reshape(b, n_real, DA_HEADS, DA_V_DIM)
    return jnp.concatenate([o_m, o_r], axis=1)


def split_chunks(t):
    meta = t[:, :, :N_META][:, :, None]
    real = t[:, :, N_META:]
    return meta, real.reshape(real.shape[:2] + (-1, CHUNK) + real.shape[3:])


def gdn_chunk_terms(q, k, v, beta, g):
    c = q.shape[-2]
    gc = jnp.cumsum(g, axis=-1)
    diff = gc[..., :, None] - gc[..., None, :]
    idx = jnp.arange(c)
    strict = idx[:, None] > idx[None, :]
    incl = idx[:, None] >= idx[None, :]
    dec_strict = jnp.exp(jnp.where(strict, diff, -jnp.inf))
    dec_incl = jnp.exp(jnp.where(incl, diff, -jnp.inf))
    m = beta[..., :, None] * jnp.einsum('bhnid,bhnjd->bhnij', k, k) * dec_strict
    lhs = m + jnp.eye(c, dtype=m.dtype)
    gamma = jnp.exp(gc)
    rhs = jnp.concatenate([beta[..., None] * v, (beta * gamma)[..., None] * k], axis=-1)
    sol = lax.linalg.triangular_solve(lhs, rhs, left_side=True, lower=True, unit_diagonal=True)
    dv = v.shape[-1]
    u_v, w_k = sol[..., :dv], sol[..., dv:]
    a_qk = jnp.einsum('bhnid,bhnjd->bhnij', q, k) * dec_incl
    k_dec = k * jnp.exp(gc[..., -1:] - gc)[..., None]
    return (q * gamma[..., None], a_qk, u_v, w_k, k_dec, gamma[..., -1])


def gdn_chunk_step(s, terms):
    q_g, a_qk, u_v, w_k, k_dec, gamma_end = terms
    w = u_v - jnp.einsum('bhck,bhkv->bhcv', w_k, s)
    o = jnp.einsum('bhck,bhkv->bhcv', q_g, s) + jnp.einsum('bhij,bhjv->bhiv', a_qk, w)
    s = gamma_end[..., None, None] * s + jnp.einsum('bhck,bhcv->bhkv', k_dec, w)
    return s, o


def gated_deltanet(p, conv_w, a_log, dt_bias, norm_gain):
    b, l = p.shape[:2]
    p = p.astype(jnp.float32)
    qkv = jax.nn.silu(causal_dwconv(p[..., :3 * GDN_W], conv_w.astype(jnp.float32)))
    q, k, v = jnp.split(qkv, 3, axis=-1)
    z = p[..., 3 * GDN_W:4 * GDN_W]
    beta = jax.nn.sigmoid(p[..., 4 * GDN_W:4 * GDN_W + GDN_HEADS])
    g = -jnp.exp(a_log.astype(jnp.float32)) * jax.nn.softplus(p[..., 4 * GDN_W + GDN_HEADS:] + dt_bias.astype(jnp.float32))

    def to_heads(t):
        return jnp.transpose(t.reshape(b, l, GDN_HEADS, GDN_HEAD_DIM), (0, 2, 1, 3))

    q = l2norm(to_heads(q)) * (GDN_HEAD_DIM ** -0.5)
    k = l2norm(to_heads(k))
    v = to_heads(v)
    beta = jnp.transpose(beta, (0, 2, 1))
    g = jnp.transpose(g, (0, 2, 1))
    meta_parts, real_parts = zip(*[split_chunks(t) for t in (q, k, v, beta, g)])
    meta_terms = gdn_chunk_terms(*meta_parts)
    real_terms = gdn_chunk_terms(*real_parts)
    s0 = jnp.zeros((b, GDN_HEADS, GDN_HEAD_DIM, GDN_HEAD_DIM), jnp.float32)
    s_meta, o_meta = gdn_chunk_step(s0, tuple(t[:, :, 0] for t in meta_terms))
    _, o_real = lax.scan(gdn_chunk_step, s_meta, tuple(jnp.moveaxis(t, 2, 0) for t in real_terms))
    o_real = jnp.moveaxis(o_real, 0, 2).reshape(b, GDN_HEADS, l - N_META, GDN_HEAD_DIM)
    o = jnp.transpose(jnp.concatenate([o_meta, o_real], axis=2), (0, 2, 1, 3))
    o = rms_norm(o, norm_gain) * jax.nn.silu(z.reshape(b, l, GDN_HEADS, GDN_HEAD_DIM))
    return o.reshape(b, l, GDN_W)


def rwkv7_time_mix(p, mu, w0, w_up, a0, a_up, g_up, k_k, k_a, r_k, gn_gain, gn_bias):
    b, l = p.shape[:2]
    f32 = jnp.float32
    p = p.astype(f32)
    ps = p + (token_shift(p) - p) * mu.astype(f32)
    i0, i1, i2 = RW_W, 2 * RW_W, 3 * RW_W
    i3, i4 = i2 + RW_DECAY_LORA, i2 + RW_DECAY_LORA + RW_A_LORA
    r, k, v = ps[..., :i0], ps[..., i0:i1], ps[..., i1:i2]
    wl, al, gl = ps[..., i2:i3], ps[..., i3:i4], ps[..., i4:]
    w_log = -jax.nn.softplus(-(w0.astype(f32) + jnp.tanh(wl) @ w_up.astype(f32))) - 0.5
    decay = jnp.exp(-jnp.exp(w_log))
    a = jax.nn.sigmoid(a0.astype(f32) + al @ a_up.astype(f32))
    gate = jax.nn.sigmoid(gl) @ g_up.astype(f32)
    hs = (b, l, RW_HEADS, RW_HEAD_DIM)
    kk = l2norm((k * k_k.astype(f32)).reshape(hs))
    k = k * (1.0 + (a - 1.0) * k_a.astype(f32))
    r, k, v, a, decay = (t.reshape(hs) for t in (r, k, v, a, decay))

    def step(s, inp):
        r_t, w_t, k_t, v_t, kk_t, a_t = inp
        s_kk = jnp.einsum('bhvk,bhk->bhv', s, kk_t)
        s = s * w_t[:, :, None, :] - s_kk[..., None] * (kk_t * a_t)[:, :, None, :] + v_t[..., None] * k_t[:, :, None, :]
        return s, jnp.einsum('bhvk,bhk->bhv', s, r_t)

    s0 = jnp.zeros((b, RW_HEADS, RW_HEAD_DIM, RW_HEAD_DIM), f32)
    xs = tuple(jnp.moveaxis(t, 1, 0) for t in (r, decay, k, v, kk, a))
    _, o = lax.scan(step, s0, xs)
    o = jnp.moveaxis(o, 0, 1)
    mean = jnp.mean(o, axis=-1, keepdims=True)
    var = jnp.mean(jnp.square(o - mean), axis=-1, keepdims=True)
    o = (o - mean) * lax.rsqrt(var + RW_GN_EPS) * gn_gain.astype(f32).reshape(RW_HEADS, RW_HEAD_DIM) \
        + gn_bias.astype(f32).reshape(RW_HEADS, RW_HEAD_DIM)
    o = o + jnp.sum(r * k * r_k.astype(f32).reshape(RW_HEADS, RW_HEAD_DIM), axis=-1, keepdims=True) * v
    return o.reshape(b, l, RW_W) * gate


def setup_inputs(seed: int = 0) -> dict:
    key = jax.random.key(seed)
    ks = iter(jax.random.split(key, 48))

    def nrm(shape, scale):
        return scale * jax.random.normal(next(ks), shape, jnp.float32)

    def gain(shape):
        return 1.0 + nrm(shape, 0.02)

    dt = jnp.exp(jax.random.uniform(next(ks), (DEPTH, GDN_HEADS), jnp.float32,
                                    minval=math.log(1e-3), maxval=math.log(1e-1)))
    return {
        'x': nrm((BATCH, SEQ, D_MODEL), 1.0),
        'meta_tokens': nrm((N_META, D_MODEL), 1.0),
        'pre_mix_gain': gain((DEPTH, D_MODEL)),
        'post_mix_gain': gain((DEPTH, D_MODEL)),
        'pre_mlp_gain': gain((DEPTH, D_MODEL)),
        'post_mlp_gain': gain((DEPTH, D_MODEL)),
        'w_in': nrm((DEPTH, D_MODEL, IN_COLS), D_MODEL ** -0.5),
        'b_gate': nrm((DEPTH, GATE_COLS), 0.01),
        'da_lambda': nrm((DEPTH, 4, DA_HEAD_DIM), 0.1),
        'da_norm_gain': gain((DEPTH, DA_V_DIM)),
        'gdn_conv': nrm((DEPTH, GDN_CONV, 3 * GDN_W), GDN_CONV ** -0.5),
        'gdn_a_log': jnp.log(jax.random.uniform(next(ks), (DEPTH, GDN_HEADS), jnp.float32, minval=1.0, maxval=16.0)),
        'gdn_dt_bias': dt + jnp.log(-jnp.expm1(-dt)),
        'gdn_norm_gain': gain((DEPTH, GDN_HEAD_DIM)),
        'rw_mu': jax.random.uniform(next(ks), (DEPTH, RW_COLS), jnp.float32),
        'rw_w0': jnp.linspace(-6.0, -1.0, RW_W, dtype=jnp.float32)[None] + nrm((DEPTH, RW_W), 0.1),
        'rw_w_up': nrm((DEPTH, RW_DECAY_LORA, RW_W), 0.1 * RW_DECAY_LORA ** -0.5),
        'rw_a0': nrm((DEPTH, RW_W), 0.1),
        'rw_a_up': nrm((DEPTH, RW_A_LORA, RW_W), RW_A_LORA ** -0.5),
        'rw_g_up': nrm((DEPTH, RW_GATE_LORA, RW_W), RW_GATE_LORA ** -0.5),
        'rw_k_k': 0.85 + nrm((DEPTH, RW_W), 0.02),
        'rw_k_a': 1.0 + nrm((DEPTH, RW_W), 0.02),
        'rw_r_k': nrm((DEPTH, RW_W), 0.1),
        'rw_gn_gain': gain((DEPTH, RW_W)),
        'rw_gn_bias': nrm((DEPTH, RW_W), 0.01),
        'w_da_out': nrm((DEPTH, DA_V_W, D_MODEL), DA_V_W ** -0.5),
        'w_gdn_out': nrm((DEPTH, GDN_W, D_MODEL), GDN_W ** -0.5),
        'w_rw_out': nrm((DEPTH, RW_W, D_MODEL), RW_W ** -0.5),
        'w_out': nrm((DEPTH, D_MODEL, D_MODEL), D_MODEL ** -0.5),
        'w_ff1': nrm((DEPTH, D_MODEL, D_FF), D_MODEL ** -0.5),
        'w_ff2': nrm((DEPTH, D_FF, D_MODEL), D_FF ** -0.5),
    }


def reference(x, meta_tokens, pre_mix_gain, post_mix_gain, pre_mlp_gain, post_mlp_gain,
              w_in, b_gate, da_lambda, da_norm_gain, gdn_conv, gdn_a_log, gdn_dt_bias, gdn_norm_gain,
              rw_mu, rw_w0, rw_w_up, rw_a0, rw_a_up, rw_g_up, rw_k_k, rw_k_a, rw_r_k, rw_gn_gain, rw_gn_bias,
              w_da_out, w_gdn_out, w_rw_out, w_out, w_ff1, w_ff2):
    b = x.shape[0]
    meta = jnp.broadcast_to(meta_tokens[None].astype(x.dtype), (b, N_META, D_MODEL))
    h = jnp.concatenate([meta, x], axis=1)
    l = h.shape[1]
    cos, sin = rope_tables(l, DA_HEAD_DIM)
    splits = [DA_COLS, DA_COLS + GDN_COLS, DA_COLS + GDN_COLS + RW_COLS]
    for i in range(DEPTH):
        xn = rms_norm(h, pre_mix_gain[i])
        proj = xn @ w_in[i]
        p_da, p_gdn, p_rw, p_gate = jnp.split(proj, splits, axis=-1)

        lam_init = 0.8 - 0.6 * math.exp(-0.3 * i)
        lv = da_lambda[i].astype(jnp.float32)
        lam = jnp.exp(jnp.sum(lv[0] * lv[1])) - jnp.exp(jnp.sum(lv[2] * lv[3])) + lam_init
        q = p_da[..., :DA_QK_W].reshape(b, l, DA_HEADS, 2, DA_HEAD_DIM)
        k = p_da[..., DA_QK_W:2 * DA_QK_W].reshape(b, l, DA_HEADS, 2, DA_HEAD_DIM)
        v = p_da[..., 2 * DA_QK_W:].reshape(b, l, DA_HEADS, DA_V_DIM)
        o_da = rms_norm(diff_attention(q, k, v, lam, cos, sin), da_norm_gain[i]) * (1.0 - lam_init)
        y_da = o_da.reshape(b, l, DA_V_W).astype(h.dtype) @ w_da_out[i]

        o_gdn = gated_deltanet(p_gdn, gdn_conv[i], gdn_a_log[i], gdn_dt_bias[i], gdn_norm_gain[i])
        y_gdn = o_gdn.astype(h.dtype) @ w_gdn_out[i]

        o_rw = rwkv7_time_mix(p_rw, rw_mu[i], rw_w0[i], rw_w_up[i], rw_a0[i], rw_a_up[i], rw_g_up[i],
                              rw_k_k[i], rw_k_a[i], rw_r_k[i], rw_gn_gain[i], rw_gn_bias[i])
        y_rw = o_rw.astype(h.dtype) @ w_rw_out[i]

        gates = jax.nn.sigmoid((p_gate + b_gate[i]).astype(jnp.float32)).astype(h.dtype)
        gates = gates.reshape(b, l, N_BRANCH, D_MODEL)
        mixed = gates[:, :, 0] * y_da + gates[:, :, 1] * y_gdn + gates[:, :, 2] * y_rw
        h = h + rms_norm(mixed @ w_out[i], post_mix_gain[i])

        xn = rms_norm(h, pre_mlp_gain[i])
        hid = jnp.square(jax.nn.relu(xn @ w_ff1[i]))
        h = h + rms_norm(hid @ w_ff2[i], post_mlp_gain[i])
    return h[:, N_META:]
```

```python
import functools
import math

import jax
import jax.numpy as jnp
from jax import lax
from jax.experimental import pallas as pl
from jax.experimental.pallas import tpu as pltpu

F32 = jnp.float32
BF16 = jnp.bfloat16
HIGHEST = lax.Precision.HIGHEST

CHUNK = 64
N_META = 16
PAD_F = CHUNK - N_META
NORM_EPS = 1e-6
ROPE_THETA = 10000.0

DA_HEADS = 4
DA_HEAD_DIM = 128
DA_V_DIM = 256
GDN_HEADS = 8
GDN_HEAD_DIM = 128
GDN_W = 1024
GDN_CONV = 4
RW_HEADS = 16
RW_HEAD_DIM = 64
RW_W = 1024
RW_DECAY_LORA = 64
RW_A_LORA = 64
RW_GATE_LORA = 160
RW_GN_EPS = RW_HEAD_DIM * 1e-5

C_DA = 0
C_GDN = 3072
C_GDN_BA = 7168
C_RW = 7296
C_RW_LORA = 10368
C_GATE = 10752
IN_COLS_P = 16896

ROW_BLOCK = 256
SUB = 16
VMEM_LIMIT = 56 * 1024 * 1024


def _pick(n, candidates):
    for c in candidates:
        if n % c == 0:
            return c
    raise ValueError(f"no block size for {n}")


def _cparams(sem):
    return pltpu.CompilerParams(dimension_semantics=sem, vmem_limit_bytes=VMEM_LIMIT)


def _rmsnorm_kernel(x_ref, g_ref, o_ref):
    x = x_ref[...]
    ms = jnp.mean(x * x, axis=-1, keepdims=True)
    o_ref[...] = (x * lax.rsqrt(ms + NORM_EPS) * g_ref[...]).astype(o_ref.dtype)


def rmsnorm_cast(x, gain):
    m, d = x.shape
    tm = _pick(m, (512, 256, 128))
    return pl.pallas_call(
        _rmsnorm_kernel,
        grid=(m // tm,),
        in_specs=[pl.BlockSpec((tm, d), lambda i: (i, 0)),
                  pl.BlockSpec((1, d), lambda i: (0, 0))],
        out_specs=pl.BlockSpec((tm, d), lambda i: (i, 0)),
        out_shape=jax.ShapeDtypeStruct((m, d), BF16),
        compiler_params=_cparams(("parallel",)),
        name="rmsnorm_cast",
    )(x, gain.reshape(1, d).astype(F32))


def _mm_kernel(x_ref, w_ref, o_ref, *, relu2):
    a = jnp.dot(x_ref[...].astype(BF16), w_ref[...], preferred_element_type=F32)
    if relu2:
        a = jnp.square(jnp.maximum(a, 0.0))
    o_ref[...] = a.astype(o_ref.dtype)


def matmul(x, w, out_dtype=F32, relu2=False, tm=None, tn=None):
    m, k = x.shape
    n = w.shape[1]
    tm = tm or _pick(m, (1280, 768, 512, 256, 128))
    tn = tn or _pick(n, (512, 256, 128))
    return pl.pallas_call(
        functools.partial(_mm_kernel, relu2=relu2),
        grid=(m // tm, n // tn),
        in_specs=[pl.BlockSpec((tm, k), lambda i, j: (i, 0)),
                  pl.BlockSpec((k, tn), lambda i, j: (0, j))],
        out_specs=pl.BlockSpec((tm, tn), lambda i, j: (i, j)),
        out_shape=jax.ShapeDtypeStruct((m, n), out_dtype),
        compiler_params=_cparams(("parallel", "arbitrary")),
        name="matmul",
    )(x, w)


def _mm_norm_res_kernel(x_ref, w_ref, h_ref, g_ref, o_ref, acc_ref, *, nk, tm, blocks_per_seq, lo, hi):
    kk = pl.program_id(1)
    base = (pl.program_id(0) % blocks_per_seq) * tm

    @pl.when(kk == 0)
    def _():
        acc_ref[...] = jnp.zeros_like(acc_ref)

    acc_ref[...] += jnp.dot(x_ref[...], w_ref[...], preferred_element_type=F32)

    @pl.when(kk == nk - 1)
    def _():
        a = acc_ref[...]
        ms = jnp.mean(a * a, axis=-1, keepdims=True)
        y = a * lax.rsqrt(ms + NORM_EPS) * g_ref[...]
        pos = base + lax.broadcasted_iota(jnp.int32, (tm, 1), 0)
        keep = (pos >= lo) & (pos < hi)
        o_ref[...] = jnp.where(keep, h_ref[...] + y, 0.0)


def mm_norm_residual(x, w, h, gain, seq_len_p, lo, hi):
    m, k = x.shape
    n = w.shape[1]
    tm = _pick(seq_len_p, (640, 256, 128))
    tk = _pick(k, (1024, 512))
    nk = k // tk
    return pl.pallas_call(
        functools.partial(_mm_norm_res_kernel, nk=nk, tm=tm, blocks_per_seq=seq_len_p // tm, lo=lo, hi=hi),
        grid=(m // tm, nk),
        in_specs=[pl.BlockSpec((tm, tk), lambda i, kk: (i, kk)),
                  pl.BlockSpec((tk, n), lambda i, kk: (kk, 0)),
                  pl.BlockSpec((tm, n), lambda i, kk: (i, 0)),
                  pl.BlockSpec((1, n), lambda i, kk: (0, 0))],
        out_specs=pl.BlockSpec((tm, n), lambda i, kk: (i, 0)),
        out_shape=jax.ShapeDtypeStruct((m, n), F32),
        scratch_shapes=[pltpu.VMEM((tm, n), F32)],
        compiler_params=_cparams(("parallel", "arbitrary")),
        name="mm_norm_residual",
    )(x, w, h, gain.reshape(1, n).astype(F32))


def _branch_mix_kernel(oa_ref, ob_ref, oc_ref, wa_ref, wb_ref, wc_ref,
                       ga_ref, gb_ref, gc_ref, ba_ref, bb_ref, bc_ref, o_ref):
    acc = None
    for o_r, w_r, g_r, b_r in ((oa_ref, wa_ref, ga_ref, ba_ref),
                               (ob_ref, wb_ref, gb_ref, bb_ref),
                               (oc_ref, wc_ref, gc_ref, bc_ref)):
        y = jnp.dot(o_r[...], w_r[...], preferred_element_type=F32)
        t = jax.nn.sigmoid(g_r[...] + b_r[...]) * y
        acc = t if acc is None else acc + t
    o_ref[...] = acc.astype(o_ref.dtype)


def branch_mix(o_da, o_gdn, o_rw, w_da, w_gdn, w_rw, proj, b_gate, d_model):
    m, k = o_da.shape
    tm = _pick(m, (1280, 768, 512, 256, 128))
    tn = 512
    nj = d_model // tn
    gate_blk0 = C_GATE // tn
    o_spec = pl.BlockSpec((tm, k), lambda i, j: (i, 0))
    w_spec = pl.BlockSpec((k, tn), lambda i, j: (0, j))

    def g_spec(b):
        return pl.BlockSpec((tm, tn), lambda i, j, b=b: (i, gate_blk0 + b * nj + j))

    def b_spec(b):
        return pl.BlockSpec((1, tn), lambda i, j, b=b: (0, b * nj + j))

    bg = b_gate.reshape(1, 3 * d_model).astype(F32)
    return pl.pallas_call(
        _branch_mix_kernel,
        grid=(m // tm, nj),
        in_specs=[o_spec, o_spec, o_spec, w_spec, w_spec, w_spec,
                  g_spec(0), g_spec(1), g_spec(2), b_spec(0), b_spec(1), b_spec(2)],
        out_specs=pl.BlockSpec((tm, tn), lambda i, j: (i, j)),
        out_shape=jax.ShapeDtypeStruct((m, d_model), BF16),
        compiler_params=_cparams(("parallel", "arbitrary")),
        name="branch_mix",
    )(o_da, o_gdn, o_rw, w_da, w_gdn, w_rw, proj, proj, proj, bg, bg, bg)


def _rope_kernel(p_ref, c_ref, s_ref, q_ref, k_ref, v_ref):
    c = c_ref[...]
    s = s_ref[...]
    scale = DA_HEAD_DIM ** -0.5
    for g in range(2 * DA_HEADS):
        sl = slice(g * DA_HEAD_DIM, (g + 1) * DA_HEAD_DIM)
        xq = p_ref[:, sl]
        q_ref[:, sl] = ((xq * c + pltpu.roll(xq, DA_HEAD_DIM // 2, 1) * s) * scale).astype(BF16)
        xk = p_ref[:, 1024 + g * DA_HEAD_DIM:1024 + (g + 1) * DA_HEAD_DIM]
        k_ref[:, sl] = (xk * c + pltpu.roll(xk, DA_HEAD_DIM // 2, 1) * s).astype(BF16)
    v_ref[...] = p_ref[:, 2048:3072].astype(BF16)


def rope_prep(proj, cos_t, sin_t, seq_len_p):
    m = proj.shape[0]
    tm = _pick(seq_len_p, (640, 256, 128))
    nb = seq_len_p // tm
    out = jax.ShapeDtypeStruct((m, 1024), BF16)
    o_spec = pl.BlockSpec((tm, 1024), lambda i: (i, 0))
    t_spec = pl.BlockSpec((tm, DA_HEAD_DIM), lambda i: (i % nb, 0))
    return pl.pallas_call(
        _rope_kernel,
        grid=(m // tm,),
        in_specs=[pl.BlockSpec((tm, 3072), lambda i: (i, 0)), t_spec, t_spec],
        out_specs=[o_spec, o_spec, o_spec],
        out_shape=[out, out, out],
        compiler_params=_cparams(("parallel",)),
        name="rope_prep",
    )(proj, cos_t, sin_t)


def _diff_attn_kernel(q_ref, k_ref, v_ref, lam_ref, gain_ref, o_ref,
                      m_ref, l_ref, acc_ref, *, tq, tk, nk, lam_init):
    iq = pl.program_id(2)
    ik = pl.program_id(3)
    last = ((iq + 1) * tq - 1) // tk

    @pl.when(ik == 0)
    def _():
        m_ref[...] = jnp.full_like(m_ref, -1e30)
        l_ref[...] = jnp.zeros_like(l_ref)
        acc_ref[...] = jnp.zeros_like(acc_ref)

    @pl.when(ik <= last)
    def _():
        qpos = iq * tq + lax.broadcasted_iota(jnp.int32, (tq, 1), 0)
        q_end = ((qpos >> 6) + 1) << 6
        kpos = ik * tk + lax.broadcasted_iota(jnp.int32, (1, tk), 1)
        vis = (kpos < q_end) & (kpos >= PAD_F)
        v = v_ref[...]
        for mp in range(2):
            sl = slice(mp * DA_HEAD_DIM, (mp + 1) * DA_HEAD_DIM)
            s = lax.dot_general(q_ref[:, sl], k_ref[:, sl], (((1,), (1,)), ((), ())),
                                preferred_element_type=F32)
            s = jnp.where(vis, s, -1e30)
            m_prev = m_ref[mp]
            m_new = jnp.maximum(m_prev, jnp.max(s, axis=1, keepdims=True))
            alpha = jnp.exp(m_prev - m_new)
            p = jnp.exp(s - m_new)
            l_ref[mp] = alpha * l_ref[mp] + jnp.sum(p, axis=1, keepdims=True)
            acc_ref[mp] = alpha * acc_ref[mp] + jnp.dot(p.astype(BF16), v, preferred_element_type=F32)
            m_ref[mp] = m_new

    @pl.when(ik == nk - 1)
    def _():
        lv = lam_ref[...]
        lam = (jnp.exp(jnp.sum(lv[0:1] * lv[1:2], axis=1, keepdims=True))
               - jnp.exp(jnp.sum(lv[2:3] * lv[3:4], axis=1, keepdims=True)) + lam_init)
        o = acc_ref[0] / l_ref[0] - lam * (acc_ref[1] / l_ref[1])
        ms = jnp.mean(o * o, axis=-1, keepdims=True)
        o = o * lax.rsqrt(ms + NORM_EPS) * gain_ref[...] * (1.0 - lam_init)
        o_ref[...] = o.astype(o_ref.dtype)


def diff_attention(q, k, v, da_lambda, da_norm_gain, batch, seq_len_p, lam_init):
    tq = _pick(seq_len_p, (640, 256, 128))
    tk = tq
    nq = seq_len_p // tq
    nk = seq_len_p // tk

    def kv_map(b, h, iq, ik):
        last = ((iq + 1) * tq - 1) // tk
        return (b * nk + jnp.minimum(ik, last), h)

    return pl.pallas_call(
        functools.partial(_diff_attn_kernel, tq=tq, tk=tk, nk=nk, lam_init=lam_init),
        grid=(batch, DA_HEADS, nq, nk),
        in_specs=[pl.BlockSpec((tq, DA_V_DIM), lambda b, h, iq, ik: (b * nq + iq, h)),
                  pl.BlockSpec((tk, DA_V_DIM), kv_map),
                  pl.BlockSpec((tk, DA_V_DIM), kv_map),
                  pl.BlockSpec((4, DA_HEAD_DIM), lambda b, h, iq, ik: (0, 0)),
                  pl.BlockSpec((1, DA_V_DIM), lambda b, h, iq, ik: (0, 0))],
        out_specs=pl.BlockSpec((tq, DA_V_DIM), lambda b, h, iq, ik: (b * nq + iq, h)),
        out_shape=jax.ShapeDtypeStruct((batch * seq_len_p, DA_HEADS * DA_V_DIM), BF16),
        scratch_shapes=[pltpu.VMEM((2, tq, 1), F32), pltpu.VMEM((2, tq, 1), F32),
                        pltpu.VMEM((2, tq, DA_V_DIM), F32)],
        compiler_params=_cparams(("parallel", "parallel", "parallel", "arbitrary")),
        name="diff_attention",
    )(q, k, v, da_lambda.astype(F32), da_norm_gain.reshape(1, DA_V_DIM).astype(F32))


def _chunk_masks(tb):
    ii = lax.broadcasted_iota(jnp.int32, (tb, tb), 0)
    jj = lax.broadcasted_iota(jnp.int32, (tb, tb), 1)
    same = (ii >> 6) == (jj >> 6)
    return ii, jj, same & (ii > jj), same & (ii >= jj)


def _bdot(a, b):
    return jnp.dot(a.astype(BF16), b.astype(BF16), preferred_element_type=F32)


def _bdot_nt(a, b):
    return lax.dot_general(a.astype(BF16), b.astype(BF16), (((1,), (1,)), ((), ())),
                           preferred_element_type=F32)


def _bdot_tn(a, b):
    return lax.dot_general(a.astype(BF16), b.astype(BF16), (((0,), (0,)), ((), ())),
                           preferred_element_type=F32)


def _unit_lower_inverse(low, ii, jj):
    sh = SUB.bit_length() - 1
    l0 = jnp.where((ii >> sh) == (jj >> sh), low, 0.0)
    x = jnp.where(ii == jj, 1.0, 0.0) - l0
    pw = l0
    for _ in range(3):
        pw = _bdot(pw, pw)
        x = x + _bdot(x, pw)
    s = SUB
    while s < CHUNK:
        sh = s.bit_length() - 1
        off = jnp.where(((ii >> (sh + 1)) == (jj >> (sh + 1))) & ((ii >> sh) > (jj >> sh)), low, 0.0)
        x = x - _bdot(_bdot(x, off), x)
        s *= 2
    return x


def _chunk_cumsum(x, incl):
    return jnp.dot(jnp.where(incl, 1.0, 0.0), x, precision=HIGHEST, preferred_element_type=F32)


def _gdn_kernel(q_ref, k_ref, v_ref, beta_ref, gcol_ref, grow_ref, o_ref, s_ref, *, tb, hp):
    @pl.when(pl.program_id(2) == 0)
    def _():
        s_ref[...] = jnp.zeros_like(s_ref)

    ii, jj, strict, incl = _chunk_masks(tb)
    upper = ((ii >> 6) == (jj >> 6)) & (ii <= jj)
    nc = tb // CHUNK
    for h in range(hp):
        sl = slice(h * GDN_HEAD_DIM, (h + 1) * GDN_HEAD_DIM)
        qh, kh, vh = q_ref[:, sl], k_ref[:, sl], v_ref[:, sl]
        beta = beta_ref[h]
        gc = jnp.sum(jnp.where(incl, grow_ref[h], 0.0), axis=1, keepdims=True)
        gr = jnp.sum(jnp.where(upper, gcol_ref[h], 0.0), axis=0, keepdims=True)
        dexp = jnp.exp(jnp.minimum(gc - gr, 0.0))
        kkt = _bdot_nt(kh, kh)
        tinv = _unit_lower_inverse(jnp.where(strict, beta * kkt * dexp, 0.0), ii, jj)
        aqk = jnp.where(incl, _bdot_nt(qh, kh) * dexp, 0.0)
        gamma = jnp.exp(gc)
        kg = kh * gamma
        s = s_ref[h]
        for c in range(nc):
            rs = slice(c * CHUNK, (c + 1) * CHUNK)
            g_last = gc[(c + 1) * CHUNK - 1:(c + 1) * CHUNK, :]
            rhs = beta[rs] * (vh[rs] - _bdot(kg[rs], s))
            w = _bdot(tinv[rs, rs], rhs)
            o_ref[rs, sl] = gamma[rs] * _bdot(qh[rs], s) + _bdot(aqk[rs, rs], w)
            k_dec = kh[rs] * jnp.exp(g_last - gc[rs])
            s = jnp.exp(g_last) * s + _bdot_tn(k_dec, w)
        s_ref[h] = s


def gdn_chunk(q, k, v, beta_col, g_col, g_row, batch, seq_len_p):
    tb = ROW_BLOCK
    hp = 2
    nb = seq_len_p // tb
    width = hp * GDN_HEAD_DIM
    x_spec = pl.BlockSpec((tb, width), lambda b, h, i: (b * nb + i, h))
    c_spec = pl.BlockSpec((hp, tb, 1), lambda b, h, i: (h, b * nb + i, 0))
    r_spec = pl.BlockSpec((hp, 1, tb), lambda b, h, i: (h, 0, b * nb + i))
    return pl.pallas_call(
        functools.partial(_gdn_kernel, tb=tb, hp=hp),
        grid=(batch, GDN_HEADS // hp, nb),
        in_specs=[x_spec, x_spec, x_spec, c_spec, c_spec, r_spec],
        out_specs=x_spec,
        out_shape=jax.ShapeDtypeStruct(q.shape, F32),
        scratch_shapes=[pltpu.VMEM((hp, GDN_HEAD_DIM, GDN_HEAD_DIM), F32)],
        compiler_params=_cparams(("parallel", "parallel", "arbitrary")),
        name="gdn_chunk",
    )(q, k, v, beta_col, g_col, g_row)


def _rwkv_kernel(r_ref, lw_ref, k_ref, v_ref, kk_ref, b_ref, o_ref, s_ref, *, tb, hp):
    @pl.when(pl.program_id(2) == 0)
    def _():
        s_ref[...] = jnp.zeros_like(s_ref)

    ii, jj, strict, incl = _chunk_masks(tb)
    nc = tb // CHUNK
    lw = lw_ref[...]
    lwc = _chunk_cumsum(lw, incl)
    w_in = jnp.exp(lwc)
    w_inv = jnp.exp(-lwc)
    rt_all = r_ref[...] * w_in
    kt_all = k_ref[...] * w_inv
    bt_all = b_ref[...] * w_inv
    kap_all = kk_ref[...] * jnp.exp(lwc - lw)
    for h in range(hp):
        sl = slice(h * RW_HEAD_DIM, (h + 1) * RW_HEAD_DIM)
        rt, kt, bt, kap = rt_all[:, sl], kt_all[:, sl], bt_all[:, sl], kap_all[:, sl]
        vh, kh, bh, lch = v_ref[:, sl], k_ref[:, sl], b_ref[:, sl], lwc[:, sl]
        tinv = _unit_lower_inverse(jnp.where(strict, _bdot_nt(kap, bt), 0.0), ii, jj)
        akk_v = _bdot(jnp.where(strict, _bdot_nt(kap, kt), 0.0), vh)
        ark_v = _bdot(jnp.where(incl, _bdot_nt(rt, kt), 0.0), vh)
        arb = jnp.where(incl, _bdot_nt(rt, bt), 0.0)
        s = s_ref[h]
        for c in range(nc):
            rs = slice(c * CHUNK, (c + 1) * CHUNK)
            p = _bdot(tinv[rs, rs], _bdot_nt(kap[rs], s) + akk_v[rs])
            o_ref[rs, sl] = _bdot_nt(rt[rs], s) + ark_v[rs] - _bdot(arb[rs, rs], p)
            l_last = lch[(c + 1) * CHUNK - 1:(c + 1) * CHUNK, :]
            dec = jnp.exp(l_last - lch[rs])
            s = s * jnp.exp(l_last) + _bdot_tn(vh[rs], kh[rs] * dec) - _bdot_tn(p, bh[rs] * dec)
        s_ref[h] = s


def rwkv_chunk(r, logw, k, v, kk, b, batch, seq_len_p):
    tb = ROW_BLOCK
    hp = 2
    nb = seq_len_p // tb
    width = hp * RW_HEAD_DIM
    x_spec = pl.BlockSpec((tb, width), lambda bi, h, i: (bi * nb + i, h))
    return pl.pallas_call(
        functools.partial(_rwkv_kernel, tb=tb, hp=hp),
        grid=(batch, RW_HEADS // hp, nb),
        in_specs=[x_spec] * 6,
        out_specs=x_spec,
        out_shape=jax.ShapeDtypeStruct(r.shape, F32),
        scratch_shapes=[pltpu.VMEM((hp, RW_HEAD_DIM, RW_HEAD_DIM), F32)],
        compiler_params=_cparams(("parallel", "parallel", "arbitrary")),
        name="rwkv_chunk",
    )(r, logw, k, v, kk, b)


def _l2norm(x, eps=1e-6):
    return x * lax.rsqrt(jnp.sum(x * x, axis=-1, keepdims=True) + eps)


def _gdn_prepare(proj, conv_w, a_log, dt_bias, batch, seq_len_p):
    m = proj.shape[0]
    qkv = proj[:, C_GDN:C_GDN + 3 * GDN_W].reshape(batch, seq_len_p, 3 * GDN_W)
    xp = jnp.pad(qkv, ((0, 0), (GDN_CONV - 1, 0), (0, 0)))
    acc = sum(xp[:, j:j + seq_len_p] * conv_w[j].astype(F32) for j in range(GDN_CONV))
    qkv = jax.nn.silu(acc).reshape(m, 3, GDN_HEADS, GDN_HEAD_DIM)
    q = (_l2norm(qkv[:, 0]) * (GDN_HEAD_DIM ** -0.5)).reshape(m, GDN_W)
    k = _l2norm(qkv[:, 1]).reshape(m, GDN_W)
    v = qkv[:, 2].reshape(m, GDN_W)
    ba = proj[:, C_GDN_BA:C_GDN_BA + 2 * GDN_HEADS]
    beta = jax.nn.sigmoid(ba[:, :GDN_HEADS])
    g = -jnp.exp(a_log.astype(F32)) * jax.nn.softplus(ba[:, GDN_HEADS:] + dt_bias.astype(F32))
    return q, k, v, beta.T[:, :, None], g.T[:, :, None], g.T[:, None, :]


def _gdn_finish(o, proj, norm_gain):
    m = o.shape[0]
    o = o.reshape(m, GDN_HEADS, GDN_HEAD_DIM)
    o = o * lax.rsqrt(jnp.mean(o * o, axis=-1, keepdims=True) + NORM_EPS) * norm_gain.astype(F32)
    z = proj[:, C_GDN + 3 * GDN_W:C_GDN + 4 * GDN_W].reshape(m, GDN_HEADS, GDN_HEAD_DIM)
    return (o * jax.nn.silu(z)).reshape(m, GDN_W).astype(BF16)


def _rwkv_prepare(proj, mu, w0, w_up, a0, a_up, g_up, k_k, k_a, batch, seq_len_p):
    m = proj.shape[0]
    n_used = 3 * RW_W + RW_DECAY_LORA + RW_A_LORA + RW_GATE_LORA
    p = jnp.concatenate([proj[:, C_RW:C_RW + 3 * RW_W],
                         proj[:, C_RW_LORA:C_RW_LORA + n_used - 3 * RW_W]], axis=1)
    p = p.reshape(batch, seq_len_p, n_used)
    prev = jnp.pad(p, ((0, 0), (1, 0), (0, 0)))[:, :-1]
    ps = (p + (prev - p) * mu.astype(F32)).reshape(m, n_used)
    i0, i1, i2 = RW_W, 2 * RW_W, 3 * RW_W
    i3, i4 = i2 + RW_DECAY_LORA, i2 + RW_DECAY_LORA + RW_A_LORA
    r, k, v = ps[:, :i0], ps[:, i0:i1], ps[:, i1:i2]
    wl, al, gl = ps[:, i2:i3], ps[:, i3:i4], ps[:, i4:]
    w_log = -jax.nn.softplus(-(w0.astype(F32) + matmul(jnp.tanh(wl), w_up.astype(BF16)))) - 0.5
    logw = -jnp.exp(w_log)
    a = jax.nn.sigmoid(a0.astype(F32) + matmul(al, a_up.astype(BF16)))
    gate = matmul(jax.nn.sigmoid(gl), g_up.astype(BF16))
    hs = (m, RW_HEADS, RW_HEAD_DIM)
    kk = _l2norm((k * k_k.astype(F32)).reshape(hs)).reshape(m, RW_W)
    k2 = k * (1.0 + (a - 1.0) * k_a.astype(F32))
    return r, logw, k2, v, kk, kk * a, gate


def _rwkv_finish(o, r, k2, v, gate, r_k, gn_gain, gn_bias):
    m = o.shape[0]
    hs = (m, RW_HEADS, RW_HEAD_DIM)
    o = o.reshape(hs)
    mean = jnp.mean(o, axis=-1, keepdims=True)
    var = jnp.mean(jnp.square(o - mean), axis=-1, keepdims=True)
    o = (o - mean) * lax.rsqrt(var + RW_GN_EPS) * gn_gain.astype(F32).reshape(RW_HEADS, RW_HEAD_DIM) \
        + gn_bias.astype(F32).reshape(RW_HEADS, RW_HEAD_DIM)
    bonus = jnp.sum((r * k2 * r_k.astype(F32)).reshape(hs), axis=-1, keepdims=True)
    o = o + bonus * v.reshape(hs)
    return (o.reshape(m, RW_W) * gate).astype(BF16)


def _pad_in_weights(w_in_i):
    d = w_in_i.shape[0]
    da_cols = 3072
    gdn_cols = 4 * GDN_W + 2 * GDN_HEADS
    rw_cols = 3 * RW_W + RW_DECAY_LORA + RW_A_LORA + RW_GATE_LORA
    o1 = da_cols
    o2 = o1 + gdn_cols
    o3 = o2 + rw_cols

    def z(n):
        return jnp.zeros((d, n), w_in_i.dtype)

    parts = [w_in_i[:, :o1],
             w_in_i[:, o1:o1 + 4 * GDN_W], w_in_i[:, o1 + 4 * GDN_W:o2], z(128 - 2 * GDN_HEADS),
             w_in_i[:, o2:o2 + 3 * RW_W], w_in_i[:, o2 + 3 * RW_W:o3], z(384 - (rw_cols - 3 * RW_W)),
             w_in_i[:, o3:]]
    return jnp.concatenate(parts, axis=1).astype(BF16)


def kernel(x, meta_tokens, pre_mix_gain, post_mix_gain, pre_mlp_gain, post_mlp_gain, w_in, b_gate, da_lambda, da_norm_gain, gdn_conv, gdn_a_log, gdn_dt_bias, gdn_norm_gain, rw_mu, rw_w0, rw_w_up, rw_a0, rw_a_up, rw_g_up, rw_k_k, rw_k_a, rw_r_k, rw_gn_gain, rw_gn_bias, w_da_out, w_gdn_out, w_rw_out, w_out, w_ff1, w_ff2):
    batch, seq, d_model = x.shape
    depth = w_in.shape[0]
    n_tok = N_META + seq
    lo, hi = PAD_F, PAD_F + n_tok
    lp = -(-hi // 1280) * 1280 if hi > 1280 else -(-hi // ROW_BLOCK) * ROW_BLOCK
    m = batch * lp

    meta = jnp.broadcast_to(meta_tokens[None].astype(x.dtype), (batch, N_META, d_model))
    h = jnp.concatenate([jnp.zeros((batch, PAD_F, d_model), x.dtype), meta, x,
                         jnp.zeros((batch, lp - hi, d_model), x.dtype)], axis=1).reshape(m, d_model)

    inv = 1.0 / (ROPE_THETA ** (jnp.arange(0, DA_HEAD_DIM, 2, dtype=F32) / DA_HEAD_DIM))
    ang = (jnp.arange(lp, dtype=F32) - PAD_F)[:, None] * inv[None, :]
    cos_t = jnp.concatenate([jnp.cos(ang), jnp.cos(ang)], axis=1)
    sin_t = jnp.concatenate([-jnp.sin(ang), jnp.sin(ang)], axis=1)

    for i in range(depth):
        lam_init = 0.8 - 0.6 * math.exp(-0.3 * i)
        xn = rmsnorm_cast(h, pre_mix_gain[i])
        proj = matmul(xn, _pad_in_weights(w_in[i]))

        q, k, v = rope_prep(proj, cos_t, sin_t, lp)
        o_da = diff_attention(q, k, v, da_lambda[i], da_norm_gain[i], batch, lp, lam_init)

        gq, gk, gv, beta_c, g_c, g_r = _gdn_prepare(proj, gdn_conv[i], gdn_a_log[i], gdn_dt_bias[i], batch, lp)
        o_gdn = _gdn_finish(gdn_chunk(gq, gk, gv, beta_c, g_c, g_r, batch, lp), proj, gdn_norm_gain[i])

        r, logw, k2, rv, kk, kb, gate = _rwkv_prepare(proj, rw_mu[i], rw_w0[i], rw_w_up[i], rw_a0[i], rw_a_up[i],
                                                      rw_g_up[i], rw_k_k[i], rw_k_a[i], batch, lp)
        o_rw = _rwkv_finish(rwkv_chunk(r, logw, k2, rv, kk, kb, batch, lp), r, k2, rv, gate,
                            rw_r_k[i], rw_gn_gain[i], rw_gn_bias[i])

        mixed = branch_mix(o_da, o_gdn, o_rw, w_da_out[i].astype(BF16), w_gdn_out[i].astype(BF16),
                           w_rw_out[i].astype(BF16), proj, b_gate[i], d_model)
        h = mm_norm_residual(mixed, w_out[i].astype(BF16), h, post_mix_gain[i], lp, lo, hi)

        xn = rmsnorm_cast(h, pre_mlp_gain[i])
        hid = matmul(xn, w_ff1[i].astype(BF16), out_dtype=BF16, relu2=True)
        h = mm_norm_residual(hid, w_ff2[i].astype(BF16), h, post_mlp_gain[i], lp, lo, hi)

    return h.reshape(batch, lp, d_model)[:, hi - seq:hi]
```

```python
import functools
import math

import numpy as np
import jax
import jax.numpy as jnp
from jax import lax
from jax.experimental import pallas as pl
from jax.experimental.pallas import tpu as pltpu

F32 = jnp.float32
BF16 = jnp.bfloat16

CHUNK = 64
N_META = 16
PAD_F = CHUNK - N_META
NORM_EPS = 1e-6
ROPE_THETA = 10000.0

DA_HEADS = 4
DA_HEAD_DIM = 128
DA_V_DIM = 256
GDN_HEADS = 8
GDN_HEAD_DIM = 128
GDN_W = 1024
GDN_CONV = 4
RW_HEADS = 16
RW_HEAD_DIM = 64
RW_W = 1024
RW_DECAY_LORA = 64
RW_A_LORA = 64
RW_GATE_LORA = 160
RW_LORA_P = 384
RW_GN_EPS = RW_HEAD_DIM * 1e-5

C_DA = 0
C_GDN = 3072
C_RW = 7168
C_GATE = 10240
C_GDN_BA = 16384
C_RW_LORA = 16512
IN_COLS_P = 16896

ROW_BLOCK = 256
GROUP_W = 256
SUB = 16
VMEM_LIMIT = 56 * 1024 * 1024


def _pick(n, candidates):
    for c in candidates:
        if n % c == 0:
            return c
    raise ValueError(f"no block size for {n}")


def _cparams(sem):
    return pltpu.CompilerParams(dimension_semantics=sem, vmem_limit_bytes=VMEM_LIMIT)


def _rmsnorm_kernel(x_ref, g_ref, o_ref):
    x = x_ref[...]
    ms = jnp.mean(x * x, axis=-1, keepdims=True)
    o_ref[...] = (x * lax.rsqrt(ms + NORM_EPS) * g_ref[...]).astype(o_ref.dtype)


def rmsnorm_cast(x, gain):
    m, d = x.shape
    tm = _pick(m, (512, 256, 128))
    return pl.pallas_call(
        _rmsnorm_kernel,
        grid=(m // tm,),
        in_specs=[pl.BlockSpec((tm, d), lambda i: (i, 0)),
                  pl.BlockSpec((1, d), lambda i: (0, 0))],
        out_specs=pl.BlockSpec((tm, d), lambda i: (i, 0)),
        out_shape=jax.ShapeDtypeStruct((m, d), BF16),
        compiler_params=_cparams(("parallel",)),
        name="rmsnorm_cast",
    )(x, gain.reshape(1, d).astype(F32))


def _mm_kernel(x_ref, w_ref, o_ref, *, relu2):
    a = jnp.dot(x_ref[...], w_ref[...], preferred_element_type=F32)
    if relu2:
        a = jnp.square(jnp.maximum(a, 0.0))
    o_ref[...] = a.astype(o_ref.dtype)


def matmul(x, w, out_dtype=F32, relu2=False):
    m, k = x.shape
    n = w.shape[1]
    tm = _pick(m, (1280, 768, 512, 256, 128))
    tn = _pick(n, (512, 256, 128))
    return pl.pallas_call(
        functools.partial(_mm_kernel, relu2=relu2),
        grid=(m // tm, n // tn),
        in_specs=[pl.BlockSpec((tm, k), lambda i, j: (i, 0)),
                  pl.BlockSpec((k, tn), lambda i, j: (0, j))],
        out_specs=pl.BlockSpec((tm, tn), lambda i, j: (i, j)),
        out_shape=jax.ShapeDtypeStruct((m, n), out_dtype),
        compiler_params=_cparams(("parallel", "arbitrary")),
        name="matmul",
    )(x, w)


def _mm_norm_res_kernel(x_ref, w_ref, h_ref, g_ref, o_ref, acc_ref, *, nk, tm, blocks_per_seq, lo, hi):
    kk = pl.program_id(1)
    base = (pl.program_id(0) % blocks_per_seq) * tm

    @pl.when(kk == 0)
    def _():
        acc_ref[...] = jnp.zeros_like(acc_ref)

    acc_ref[...] += jnp.dot(x_ref[...], w_ref[...], preferred_element_type=F32)

    @pl.when(kk == nk - 1)
    def _():
        a = acc_ref[...]
        ms = jnp.mean(a * a, axis=-1, keepdims=True)
        y = a * lax.rsqrt(ms + NORM_EPS) * g_ref[...]
        pos = base + lax.broadcasted_iota(jnp.int32, (tm, 1), 0)
        keep = (pos >= lo) & (pos < hi)
        o_ref[...] = jnp.where(keep, h_ref[...] + y, 0.0)


def mm_norm_residual(x, w, h, gain, seq_len_p, lo, hi):
    m, k = x.shape
    n = w.shape[1]
    tm = _pick(seq_len_p, (640, 256, 128))
    tk = _pick(k, (1024, 512))
    nk = k // tk
    return pl.pallas_call(
        functools.partial(_mm_norm_res_kernel, nk=nk, tm=tm, blocks_per_seq=seq_len_p // tm, lo=lo, hi=hi),
        grid=(m // tm, nk),
        in_specs=[pl.BlockSpec((tm, tk), lambda i, kk: (i, kk)),
                  pl.BlockSpec((tk, n), lambda i, kk: (kk, 0)),
                  pl.BlockSpec((tm, n), lambda i, kk: (i, 0)),
                  pl.BlockSpec((1, n), lambda i, kk: (0, 0))],
        out_specs=pl.BlockSpec((tm, n), lambda i, kk: (i, 0)),
        out_shape=jax.ShapeDtypeStruct((m, n), F32),
        scratch_shapes=[pltpu.VMEM((tm, n), F32)],
        compiler_params=_cparams(("parallel", "arbitrary")),
        name="mm_norm_residual",
    )(x, w, h, gain.reshape(1, n).astype(F32))


def _branch_mix_kernel(oa_ref, ob_ref, oc_ref, wa_ref, wb_ref, wc_ref,
                       ga_ref, gb_ref, gc_ref, ba_ref, bb_ref, bc_ref, o_ref):
    acc = None
    for o_r, w_r, g_r, b_r in ((oa_ref, wa_ref, ga_ref, ba_ref),
                               (ob_ref, wb_ref, gb_ref, bb_ref),
                               (oc_ref, wc_ref, gc_ref, bc_ref)):
        y = jnp.dot(o_r[...], w_r[...], preferred_element_type=F32)
        t = jax.nn.sigmoid(g_r[...] + b_r[...]) * y
        acc = t if acc is None else acc + t
    o_ref[...] = acc.astype(o_ref.dtype)


def branch_mix(o_da, o_gdn, o_rw, w_da, w_gdn, w_rw, proj, b_gate, d_model):
    m, k = o_da.shape
    tm = _pick(m, (1280, 768, 512, 256, 128))
    tn = 512
    nj = d_model // tn
    gate_blk0 = C_GATE // tn
    o_spec = pl.BlockSpec((tm, k), lambda i, j: (i, 0))
    w_spec = pl.BlockSpec((k, tn), lambda i, j: (0, j))

    def g_spec(b):
        return pl.BlockSpec((tm, tn), lambda i, j, b=b: (i, gate_blk0 + b * nj + j))

    def b_spec(b):
        return pl.BlockSpec((1, tn), lambda i, j, b=b: (0, b * nj + j))

    bg = b_gate.reshape(1, 3 * d_model).astype(F32)
    return pl.pallas_call(
        _branch_mix_kernel,
        grid=(m // tm, nj),
        in_specs=[o_spec, o_spec, o_spec, w_spec, w_spec, w_spec,
                  g_spec(0), g_spec(1), g_spec(2), b_spec(0), b_spec(1), b_spec(2)],
        out_specs=pl.BlockSpec((tm, tn), lambda i, j: (i, j)),
        out_shape=jax.ShapeDtypeStruct((m, d_model), BF16),
        compiler_params=_cparams(("parallel", "arbitrary")),
        name="branch_mix",
    )(o_da, o_gdn, o_rw, w_da, w_gdn, w_rw, proj, proj, proj, bg, bg, bg)


def _rope_kernel(p_ref, c_ref, s_ref, q_ref, k_ref, v_ref):
    c = c_ref[...]
    s = s_ref[...]
    scale = DA_HEAD_DIM ** -0.5 * math.log2(math.e)
    for g in range(2 * DA_HEADS):
        sl = slice(g * DA_HEAD_DIM, (g + 1) * DA_HEAD_DIM)
        xq = p_ref[:, sl]
        q_ref[:, sl] = ((xq * c + pltpu.roll(xq, DA_HEAD_DIM // 2, 1) * s) * scale).astype(BF16)
        xk = p_ref[:, 1024 + g * DA_HEAD_DIM:1024 + (g + 1) * DA_HEAD_DIM]
        k_ref[:, sl] = (xk * c + pltpu.roll(xk, DA_HEAD_DIM // 2, 1) * s).astype(BF16)
    v_ref[...] = p_ref[:, 2048:3072].astype(BF16)


def rope_prep(proj, cos_t, sin_t, seq_len_p):
    m = proj.shape[0]
    tm = _pick(seq_len_p, (640, 256, 128))
    nb = seq_len_p // tm
    out = jax.ShapeDtypeStruct((m, 1024), BF16)
    o_spec = pl.BlockSpec((tm, 1024), lambda i: (i, 0))
    t_spec = pl.BlockSpec((tm, DA_HEAD_DIM), lambda i: (i % nb, 0))
    return pl.pallas_call(
        _rope_kernel,
        grid=(m // tm,),
        in_specs=[pl.BlockSpec((tm, 3072), lambda i: (i, 0)), t_spec, t_spec],
        out_specs=[o_spec, o_spec, o_spec],
        out_shape=[out, out, out],
        compiler_params=_cparams(("parallel",)),
        name="rope_prep",
    )(proj, cos_t, sin_t)


def _diff_attn_kernel(iq_tab, ik_tab, q_ref, k_ref, v_ref, lam_ref, gain_ref, o_ref,
                      m_ref, l_ref, acc_ref, *, tq, tk, lam_init):
    t = pl.program_id(2)
    iq = iq_tab[t]
    ik = ik_tab[t]
    last = ((iq + 1) * tq - 1) // tk

    @pl.when(ik == 0)
    def _():
        m_ref[...] = jnp.full_like(m_ref, -1e30)
        l_ref[...] = jnp.zeros_like(l_ref)
        acc_ref[...] = jnp.zeros_like(acc_ref)

    def step(masked):
        if masked:
            qpos = iq * tq + lax.broadcasted_iota(jnp.int32, (tq, 1), 0)
            q_end = ((qpos >> 6) + 1) << 6
            kpos = ik * tk + lax.broadcasted_iota(jnp.int32, (1, tk), 1)
            vis = (kpos < q_end) & (kpos >= PAD_F)
        v = v_ref[...]
        for mp in range(2):
            sl = slice(mp * DA_HEAD_DIM, (mp + 1) * DA_HEAD_DIM)
            s = lax.dot_general(q_ref[:, sl], k_ref[:, sl], (((1,), (1,)), ((), ())),
                                preferred_element_type=F32)
            if masked:
                s = jnp.where(vis, s, -1e30)
            m_prev = m_ref[mp]
            m_new = jnp.maximum(m_prev, jnp.max(s, axis=1, keepdims=True))
            alpha = jnp.exp2(m_prev - m_new)
            p = jnp.exp2(s - m_new)
            l_ref[mp] = alpha * l_ref[mp] + jnp.sum(p, axis=1, keepdims=True)
            acc_ref[mp] = alpha * acc_ref[mp] + jnp.dot(p.astype(BF16), v, preferred_element_type=F32)
            m_ref[mp] = m_new

    needs_mask = (ik == 0) | ((ik + 1) * tk > iq * tq + CHUNK)

    @pl.when(needs_mask)
    def _():
        step(True)

    @pl.when(jnp.logical_not(needs_mask))
    def _():
        step(False)

    @pl.when(ik == last)
    def _():
        lv = lam_ref[...]
        lam = (jnp.exp(jnp.sum(lv[0:1] * lv[1:2], axis=1, keepdims=True))
               - jnp.exp(jnp.sum(lv[2:3] * lv[3:4], axis=1, keepdims=True)) + lam_init)
        o = acc_ref[0] / l_ref[0] - lam * (acc_ref[1] / l_ref[1])
        ms = jnp.mean(o * o, axis=-1, keepdims=True)
        o = o * lax.rsqrt(ms + NORM_EPS) * gain_ref[...] * (1.0 - lam_init)
        o_ref[...] = o.astype(o_ref.dtype)


def diff_attention(q, k, v, da_lambda, da_norm_gain, batch, seq_len_p, lam_init):
    tq = _pick(seq_len_p, (640, 256, 128))
    tk = 2 * tq if seq_len_p % (2 * tq) == 0 else tq
    nq = seq_len_p // tq
    nk = seq_len_p // tk
    pairs = [(a, b) for a in range(nq) for b in range(((a + 1) * tq - 1) // tk + 1)]
    iq_tab = jnp.asarray(np.array([p[0] for p in pairs], np.int32))
    ik_tab = jnp.asarray(np.array([p[1] for p in pairs], np.int32))

    def q_map(b, h, t, iq_t, ik_t):
        return (b * nq + iq_t[t], h)

    def kv_map(b, h, t, iq_t, ik_t):
        return (b * nk + ik_t[t], h)

    grid_spec = pltpu.PrefetchScalarGridSpec(
        num_scalar_prefetch=2,
        grid=(batch, DA_HEADS, len(pairs)),
        in_specs=[pl.BlockSpec((tq, DA_V_DIM), q_map),
                  pl.BlockSpec((tk, DA_V_DIM), kv_map),
                  pl.BlockSpec((tk, DA_V_DIM), kv_map),
                  pl.BlockSpec((4, DA_HEAD_DIM), lambda b, h, t, iq_t, ik_t: (0, 0)),
                  pl.BlockSpec((1, DA_V_DIM), lambda b, h, t, iq_t, ik_t: (0, 0))],
        out_specs=pl.BlockSpec((tq, DA_V_DIM), q_map),
        scratch_shapes=[pltpu.VMEM((2, tq, 1), F32), pltpu.VMEM((2, tq, 1), F32),
                        pltpu.VMEM((2, tq, DA_V_DIM), F32)],
    )
    return pl.pallas_call(
        functools.partial(_diff_attn_kernel, tq=tq, tk=tk, lam_init=lam_init),
        grid_spec=grid_spec,
        out_shape=jax.ShapeDtypeStruct((batch * seq_len_p, DA_HEADS * DA_V_DIM), BF16),
        compiler_params=_cparams(("parallel", "parallel", "arbitrary")),
        name="diff_attention",
    )(iq_tab, ik_tab, q, k, v, da_lambda.astype(F32), da_norm_gain.reshape(1, DA_V_DIM).astype(F32))


def _bdot(a, b):
    return jnp.dot(a.astype(BF16), b.astype(BF16), preferred_element_type=F32)


def _bdot_nt(a, b):
    return lax.dot_general(a.astype(BF16), b.astype(BF16), (((1,), (1,)), ((), ())),
                           preferred_element_type=F32)


def _bdot_tn(a, b):
    return lax.dot_general(a.astype(BF16), b.astype(BF16), (((0,), (0,)), ((), ())),
                           preferred_element_type=F32)


def _split_dot(x, w_exact, pieces):
    acc = None
    rem = x
    for _ in range(pieces):
        part = rem.astype(BF16)
        d = jnp.dot(part, w_exact, preferred_element_type=F32)
        acc = d if acc is None else acc + d
        rem = rem - part.astype(F32)
    return acc


def _split_dot_left(w_exact, x, pieces):
    acc = None
    rem = x
    for _ in range(pieces):
        part = rem.astype(BF16)
        d = jnp.dot(w_exact, part, preferred_element_type=F32)
        acc = d if acc is None else acc + d
        rem = rem - part.astype(F32)
    return acc


def _unit_lower_inverse(low, ii, jj):
    sh = SUB.bit_length() - 1
    l0 = jnp.where((ii >> sh) == (jj >> sh), low, 0.0)
    x = jnp.where(ii == jj, 1.0, 0.0) - l0
    pw = l0
    for _ in range(3):
        pw = _bdot(pw, pw)
        x = x + _bdot(x, pw)
    s = SUB
    while s < CHUNK:
        sh = s.bit_length() - 1
        off = jnp.where(((ii >> (sh + 1)) == (jj >> (sh + 1))) & ((ii >> sh) > (jj >> sh)), low, 0.0)
        x = x - _bdot(_bdot(x, off), x)
        s *= 2
    return x


def _rows_of_chunk_last(x, tb):
    parts = []
    for c in range(tb // CHUNK):
        last = x[(c + 1) * CHUNK - 1:(c + 1) * CHUNK, :]
        parts.append(jnp.broadcast_to(last, (CHUNK, x.shape[1])))
    return jnp.concatenate(parts, axis=0)


def _shift_rows(x, prev8, s, row8):
    rolled = pltpu.roll(x, s, 0)
    first8 = jnp.where(row8 < s, pltpu.roll(prev8, s, 0), rolled[:8])
    return jnp.concatenate([first8, rolled[8:]], axis=0)


def _gdn_kernel(pq_ref, pk_ref, pv_ref, pz_ref, pba_ref, cq_ref, ck_ref, cv_ref, gv_ref, ng_ref,
                o_ref, s_ref, halo_ref, *, tb):
    grp = pl.program_id(1)

    @pl.when(pl.program_id(2) == 0)
    def _():
        s_ref[...] = jnp.zeros_like(s_ref)
        halo_ref[...] = jnp.zeros_like(halo_ref)

    hd = GDN_HEAD_DIM
    nh = GROUP_W // hd
    ii = lax.broadcasted_iota(jnp.int32, (tb, tb), 0)
    jj = lax.broadcasted_iota(jnp.int32, (tb, tb), 1)
    same = (ii >> 6) == (jj >> 6)
    strict = same & (ii > jj)
    incl = same & (ii >= jj)
    upper = same & (ii <= jj)
    eye = ii == jj
    row8 = lax.broadcasted_iota(jnp.int32, (8, 1), 0)
    gi = lax.broadcasted_iota(jnp.int32, (GROUP_W, GROUP_W), 0)
    gj = lax.broadcasted_iota(jnp.int32, (GROUP_W, GROUP_W), 1)
    head_bd = (gi >> 7) == (gj >> 7)
    ones_bd = jnp.where(head_bd, 1.0, 0.0).astype(BF16)

    def conv_silu(x_ref, idx, cw_ref):
        x = x_ref[...]
        prev8 = halo_ref[idx]
        cw = cw_ref[...]
        acc = x * cw[GDN_CONV - 1:GDN_CONV]
        for s in range(1, GDN_CONV):
            acc = acc + _shift_rows(x, prev8, s, row8) * cw[GDN_CONV - 1 - s:GDN_CONV - s]
        halo_ref[idx] = x[tb - 8:, :]
        return acc * jax.nn.sigmoid(acc)

    def l2n(x):
        return x * lax.rsqrt(_split_dot(x * x, ones_bd, 2) + 1e-6)

    q = l2n(conv_silu(pq_ref, 0, cq_ref)) * (hd ** -0.5)
    k = l2n(conv_silu(pk_ref, 1, ck_ref))
    v = conv_silu(pv_ref, 2, cv_ref)

    ba = pba_ref[...]
    lane = lax.broadcasted_iota(jnp.int32, (1, 128), 1)
    beta_all = jax.nn.sigmoid(ba)
    xa = ba + gv_ref[1:2, :]
    softplus = jnp.maximum(xa, 0.0) + jnp.log(1.0 + jnp.exp(-jnp.abs(xa)))
    g_all = -jnp.exp(gv_ref[0:1, :]) * softplus

    wk_parts, kw_parts, w0_parts, qq_parts, o0_parts, kdec_parts, gend = [], [], [], [], [], [], []
    for hh in range(nh):
        head = grp * nh + hh
        sl = slice(hh * hd, (hh + 1) * hd)
        qh, kh, vh = q[:, sl], k[:, sl], v[:, sl]
        beta = jnp.sum(jnp.where(lane == head, beta_all, 0.0), axis=1, keepdims=True)
        gcol = jnp.sum(jnp.where(lane == GDN_HEADS + head, g_all, 0.0), axis=1, keepdims=True)
        gr = jnp.sum(jnp.where(upper, gcol, 0.0), axis=0, keepdims=True)
        gc = jnp.sum(jnp.where(eye, gr, 0.0), axis=1, keepdims=True)
        dexp = jnp.exp(jnp.minimum(gc - gr, 0.0))
        tinv = _unit_lower_inverse(jnp.where(strict, beta * _bdot_nt(kh, kh) * dexp, 0.0), ii, jj)
        aqk = jnp.where(incl, _bdot_nt(qh, kh) * dexp, 0.0)
        gamma = jnp.exp(gc)
        wk = _bdot(tinv, jnp.concatenate([beta * vh, (beta * gamma) * kh], axis=1))
        aq = _bdot(aqk, wk)
        w0_parts.append(wk[:, :hd])
        kw_parts.append(wk[:, hd:])
        o0_parts.append(aq[:, :hd])
        qq_parts.append(gamma * qh - aq[:, hd:])
        g_last = _rows_of_chunk_last(gc, tb)
        kdec_parts.append(kh * jnp.exp(g_last - gc))
        gend.append(jnp.exp(g_last))

    w0 = jnp.concatenate(w0_parts, axis=1)
    kw = jnp.concatenate(kw_parts, axis=1)
    o0 = jnp.concatenate(o0_parts, axis=1)
    qq = jnp.concatenate(qq_parts, axis=1)
    kdec = jnp.concatenate(kdec_parts, axis=1)
    srow = lax.broadcasted_iota(jnp.int32, (GROUP_W, 1), 0) >> 7

    s = s_ref[...]
    o_chunks = []
    for c in range(tb // CHUNK):
        rs = slice(c * CHUNK, (c + 1) * CHUNK)
        x = _bdot_tn(kdec[rs], jnp.concatenate([kw[rs], w0[rs]], axis=1))
        mcorr = jnp.where(head_bd, x[:, :GROUP_W], 0.0)
        u = jnp.where(head_bd, x[:, GROUP_W:], 0.0)
        o_chunks.append(_bdot(qq[rs], s) + o0[rs])
        decay = gend[0][c * CHUNK:c * CHUNK + 1, :]
        for hh in range(1, nh):
            decay = jnp.where(srow == hh, gend[hh][c * CHUNK:c * CHUNK + 1, :], decay)
        s = decay * s - _bdot(mcorr, s) + u
    s_ref[...] = s

    o = jnp.concatenate(o_chunks, axis=0)
    ms = _split_dot(o * o, ones_bd, 2) * (1.0 / hd)
    z = pz_ref[...]
    o = o * lax.rsqrt(ms + NORM_EPS) * ng_ref[...] * (z * jax.nn.sigmoid(z))
    o_ref[...] = o.astype(o_ref.dtype)


def gdn_branch(proj, conv_w, a_log, dt_bias, norm_gain, batch, seq_len_p):
    tb = ROW_BLOCK
    nb = seq_len_p // tb
    ng = GDN_W // GROUP_W
    c0 = C_GDN // GROUP_W

    def p_spec(off):
        return pl.BlockSpec((tb, GROUP_W), lambda b, g, i, off=off: (b * nb + i, c0 + off * ng + g))

    def c_spec(off):
        return pl.BlockSpec((GDN_CONV, GROUP_W), lambda b, g, i, off=off: (0, off * ng + g))

    gvec = jnp.zeros((8, 128), F32)
    gvec = gvec.at[0, GDN_HEADS:2 * GDN_HEADS].set(a_log.astype(F32))
    gvec = gvec.at[1, GDN_HEADS:2 * GDN_HEADS].set(dt_bias.astype(F32))
    ngain = jnp.tile(norm_gain.astype(F32), GROUP_W // GDN_HEAD_DIM).reshape(1, GROUP_W)
    cw = conv_w.astype(F32)
    return pl.pallas_call(
        functools.partial(_gdn_kernel, tb=tb),
        grid=(batch, ng, nb),
        in_specs=[p_spec(0), p_spec(1), p_spec(2), p_spec(3),
                  pl.BlockSpec((tb, 128), lambda b, g, i: (b * nb + i, C_GDN_BA // 128)),
                  c_spec(0), c_spec(1), c_spec(2),
                  pl.BlockSpec((8, 128), lambda b, g, i: (0, 0)),
                  pl.BlockSpec((1, GROUP_W), lambda b, g, i: (0, 0))],
        out_specs=pl.BlockSpec((tb, GROUP_W), lambda b, g, i: (b * nb + i, g)),
        out_shape=jax.ShapeDtypeStruct((batch * seq_len_p, GDN_W), BF16),
        scratch_shapes=[pltpu.VMEM((GROUP_W, GROUP_W), F32), pltpu.VMEM((3, 8, GROUP_W), F32)],
        compiler_params=_cparams(("parallel", "parallel", "arbitrary")),
        name="gdn_branch",
    )(proj, proj, proj, proj, proj, cw, cw, cw, gvec, ngain)


def _rwkv_kernel(pr_ref, pk_ref, pv_ref, plo_ref, mur_ref, muk_ref, muv_ref, mul_ref, vec_ref,
                 wup_ref, aup_ref, gup_ref, o_ref, s_ref, prev_ref, prevl_ref, *, tb):
    @pl.when(pl.program_id(2) == 0)
    def _():
        s_ref[...] = jnp.zeros_like(s_ref)
        prev_ref[...] = jnp.zeros_like(prev_ref)
        prevl_ref[...] = jnp.zeros_like(prevl_ref)

    hd = RW_HEAD_DIM
    nh = GROUP_W // hd
    ii = lax.broadcasted_iota(jnp.int32, (tb, tb), 0)
    jj = lax.broadcasted_iota(jnp.int32, (tb, tb), 1)
    same = (ii >> 6) == (jj >> 6)
    strict = same & (ii > jj)
    incl = same & (ii >= jj)
    tri = jnp.where(incl, 1.0, 0.0).astype(BF16)
    gi = lax.broadcasted_iota(jnp.int32, (GROUP_W, GROUP_W), 0)
    gj = lax.broadcasted_iota(jnp.int32, (GROUP_W, GROUP_W), 1)
    head_bd = (gi >> 6) == (gj >> 6)
    ones_bd = jnp.where(head_bd, 1.0, 0.0).astype(BF16)
    lane_head = lax.broadcasted_iota(jnp.int32, (1, GROUP_W), 1) >> 6
    row0 = lax.broadcasted_iota(jnp.int32, (tb, 1), 0) == 0

    def lerp_shift(x_ref, prev, mu_ref):
        x = x_ref[...]
        xs = jnp.where(row0, prev[7:8, :], pltpu.roll(x, 1, 0))
        return x + (xs - x) * mu_ref[...], x[tb - 8:, :]

    r, prev_r = lerp_shift(pr_ref, prev_ref[0], mur_ref)
    kx, prev_k = lerp_shift(pk_ref, prev_ref[1], muk_ref)
    v, prev_v = lerp_shift(pv_ref, prev_ref[2], muv_ref)
    lo, prev_l = lerp_shift(plo_ref, prevl_ref[...], mul_ref)
    prev_ref[0] = prev_r
    prev_ref[1] = prev_k
    prev_ref[2] = prev_v
    prevl_ref[...] = prev_l

    w0, a0, k_k, k_a = vec_ref[0:1, :], vec_ref[1:2, :], vec_ref[2:3, :], vec_ref[3:4, :]
    r_k, gn_g, gn_b = vec_ref[4:5, :], vec_ref[5:6, :], vec_ref[6:7, :]
    zdec = w0 + _bdot(jnp.tanh(lo[:, :RW_DECAY_LORA]), wup_ref[...])
    lw = jax.nn.sigmoid(zdec) * (-math.exp(-0.5))
    a = jax.nn.sigmoid(a0 + _bdot(lo[:, RW_DECAY_LORA:RW_DECAY_LORA + RW_A_LORA], aup_ref[...]))
    gate = _bdot(jax.nn.sigmoid(lo[:, RW_DECAY_LORA + RW_A_LORA:]), gup_ref[...])
    kkx = kx * k_k
    kk = kkx * lax.rsqrt(_split_dot(kkx * kkx, ones_bd, 2) + 1e-6)
    k2 = kx * (1.0 + (a - 1.0) * k_a)
    b = kk * a

    lwc = _split_dot_left(tri, lw, 3)
    w_inv = jnp.exp(-lwc)
    rt = r * jnp.exp(lwc)
    kt = k2 * w_inv
    bt = b * w_inv
    kap = kk * jnp.exp(lwc - lw)
    l_last = _rows_of_chunk_last(lwc, tb)
    dec = jnp.exp(l_last - lwc)
    kw_end = k2 * dec
    bw_end = b * dec
    w_end = jnp.exp(l_last)

    kp = jnp.zeros((tb, GROUP_W), F32)
    p0 = jnp.zeros((tb, GROUP_W), F32)
    rq = rt
    o0 = jnp.zeros((tb, GROUP_W), F32)
    for h in range(nh):
        hm = lane_head == h
        kap_m = jnp.where(hm, kap, 0.0)
        rt_m = jnp.where(hm, rt, 0.0)
        v_m = jnp.where(hm, v, 0.0)
        tinv = _unit_lower_inverse(jnp.where(strict, _bdot_nt(kap_m, bt), 0.0), ii, jj)
        akk_v = _bdot(jnp.where(strict, _bdot_nt(kap_m, kt), 0.0), v_m)
        a_rk = jnp.where(incl, _bdot_nt(rt_m, kt), 0.0)
        a_rb = jnp.where(incl, _bdot_nt(rt_m, bt), 0.0)
        kp_h = _bdot(tinv, kap_m)
        p0_h = _bdot(tinv, akk_v)
        kp = kp + kp_h
        p0 = p0 + p0_h
        rq = rq - _bdot(a_rb, kp_h)
        o0 = o0 + _bdot(a_rk, v_m) - _bdot(a_rb, p0_h)

    s = s_ref[...]
    o_chunks = []
    for c in range(tb // CHUNK):
        rs = slice(c * CHUNK, (c + 1) * CHUNK)
        mcorr = jnp.where(head_bd, _bdot_tn(kp[rs], bw_end[rs]), 0.0)
        u = jnp.where(head_bd, _bdot_tn(jnp.concatenate([v[rs], p0[rs]], axis=0),
                                        jnp.concatenate([kw_end[rs], -bw_end[rs]], axis=0)), 0.0)
        o_chunks.append(_bdot_nt(rq[rs], s) + o0[rs])
        s = s * w_end[c * CHUNK:c * CHUNK + 1, :] - _bdot(s, mcorr) + u
    s_ref[...] = s

    o = jnp.concatenate(o_chunks, axis=0)
    mean = _split_dot(o, ones_bd, 2) * (1.0 / hd)
    d = o - mean
    var = _split_dot(d * d, ones_bd, 2) * (1.0 / hd)
    o = d * lax.rsqrt(var + RW_GN_EPS) * gn_g + gn_b
    o = o + _split_dot(r * k2 * r_k, ones_bd, 2) * v
    o_ref[...] = (o * gate).astype(o_ref.dtype)


def rwkv_branch(proj, mu, w0, w_up, a0, a_up, g_up, k_k, k_a, r_k, gn_gain, gn_bias, batch, seq_len_p):
    tb = ROW_BLOCK
    nb = seq_len_p // tb
    ng = RW_W // GROUP_W
    c0 = C_RW // GROUP_W
    n_lora = RW_DECAY_LORA + RW_A_LORA + RW_GATE_LORA

    def p_spec(off):
        return pl.BlockSpec((tb, GROUP_W), lambda b, g, i, off=off: (b * nb + i, c0 + off * ng + g))

    def mu_spec(off):
        return pl.BlockSpec((1, GROUP_W), lambda b, g, i, off=off: (0, off * ng + g))

    def w_spec(rows):
        return pl.BlockSpec((rows, GROUP_W), lambda b, g, i: (0, g))

    mu = mu.astype(F32)
    mu_rkv = mu[:3 * RW_W].reshape(1, 3 * RW_W)
    mu_l = jnp.pad(mu[3 * RW_W:], (0, RW_LORA_P - n_lora)).reshape(1, RW_LORA_P)
    vec = jnp.stack([w0, a0, k_k, k_a, r_k, gn_gain, gn_bias, jnp.zeros_like(w0)]).astype(F32)
    g_up_p = jnp.pad(g_up, ((0, RW_LORA_P - RW_DECAY_LORA - RW_A_LORA - RW_GATE_LORA), (0, 0))).astype(BF16)
    return pl.pallas_call(
        functools.partial(_rwkv_kernel, tb=tb),
        grid=(batch, ng, nb),
        in_specs=[p_spec(0), p_spec(1), p_spec(2),
                  pl.BlockSpec((tb, RW_LORA_P), lambda b, g, i: (b * nb + i, C_RW_LORA // RW_LORA_P)),
                  mu_spec(0), mu_spec(1), mu_spec(2),
                  pl.BlockSpec((1, RW_LORA_P), lambda b, g, i: (0, 0)),
                  w_spec(8), w_spec(RW_DECAY_LORA), w_spec(RW_A_LORA),
                  w_spec(RW_LORA_P - RW_DECAY_LORA - RW_A_LORA)],
        out_specs=pl.BlockSpec((tb, GROUP_W), lambda b, g, i: (b * nb + i, g)),
        out_shape=jax.ShapeDtypeStruct((batch * seq_len_p, RW_W), BF16),
        scratch_shapes=[pltpu.VMEM((GROUP_W, GROUP_W), F32), pltpu.VMEM((3, 8, GROUP_W), F32),
                        pltpu.VMEM((8, RW_LORA_P), F32)],
        compiler_params=_cparams(("parallel", "parallel", "arbitrary")),
        name="rwkv_branch",
    )(proj, proj, proj, proj, mu_rkv, mu_rkv, mu_rkv, mu_l, vec,
      w_up.astype(BF16), a_up.astype(BF16), g_up_p)


def _pad_in_weights(w_in_i):
    d = w_in_i.shape[0]
    gdn_cols = 4 * GDN_W + 2 * GDN_HEADS
    rw_cols = 3 * RW_W + RW_DECAY_LORA + RW_A_LORA + RW_GATE_LORA
    o1 = 3072
    o2 = o1 + gdn_cols
    o3 = o2 + rw_cols

    def z(n):
        return jnp.zeros((d, n), w_in_i.dtype)

    parts = [w_in_i[:, :o1],
             w_in_i[:, o1:o1 + 4 * GDN_W],
             w_in_i[:, o2:o2 + 3 * RW_W],
             w_in_i[:, o3:],
             w_in_i[:, o1 + 4 * GDN_W:o2], z(128 - 2 * GDN_HEADS),
             w_in_i[:, o2 + 3 * RW_W:o3], z(RW_LORA_P - (rw_cols - 3 * RW_W))]
    return jnp.concatenate(parts, axis=1).astype(BF16)


def kernel(x, meta_tokens, pre_mix_gain, post_mix_gain, pre_mlp_gain, post_mlp_gain, w_in, b_gate, da_lambda, da_norm_gain, gdn_conv, gdn_a_log, gdn_dt_bias, gdn_norm_gain, rw_mu, rw_w0, rw_w_up, rw_a0, rw_a_up, rw_g_up, rw_k_k, rw_k_a, rw_r_k, rw_gn_gain, rw_gn_bias, w_da_out, w_gdn_out, w_rw_out, w_out, w_ff1, w_ff2):
    batch, seq, d_model = x.shape
    depth = w_in.shape[0]
    n_tok = N_META + seq
    lo, hi = PAD_F, PAD_F + n_tok
    lp = -(-hi // 1280) * 1280 if hi > 1280 else -(-hi // ROW_BLOCK) * ROW_BLOCK
    m = batch * lp

    meta = jnp.broadcast_to(meta_tokens[None].astype(x.dtype), (batch, N_META, d_model))
    h = jnp.concatenate([jnp.zeros((batch, PAD_F, d_model), x.dtype), meta, x,
                         jnp.zeros((batch, lp - hi, d_model), x.dtype)], axis=1).reshape(m, d_model)

    inv = 1.0 / (ROPE_THETA ** (jnp.arange(0, DA_HEAD_DIM, 2, dtype=F32) / DA_HEAD_DIM))
    ang = (jnp.arange(lp, dtype=F32) - PAD_F)[:, None] * inv[None, :]
    cos_t = jnp.concatenate([jnp.cos(ang), jnp.cos(ang)], axis=1)
    sin_t = jnp.concatenate([-jnp.sin(ang), jnp.sin(ang)], axis=1)

    for i in range(depth):
        lam_init = 0.8 - 0.6 * math.exp(-0.3 * i)
        xn = rmsnorm_cast(h, pre_mix_gain[i])
        proj = matmul(xn, _pad_in_weights(w_in[i]))

        q, k, v = rope_prep(proj, cos_t, sin_t, lp)
        o_da = diff_attention(q, k, v, da_lambda[i], da_norm_gain[i], batch, lp, lam_init)
        o_gdn = gdn_branch(proj, gdn_conv[i], gdn_a_log[i], gdn_dt_bias[i], gdn_norm_gain[i], batch, lp)
        o_rw = rwkv_branch(proj, rw_mu[i], rw_w0[i], rw_w_up[i], rw_a0[i], rw_a_up[i], rw_g_up[i],
                           rw_k_k[i], rw_k_a[i], rw_r_k[i], rw_gn_gain[i], rw_gn_bias[i], batch, lp)

        mixed = branch_mix(o_da, o_gdn, o_rw, w_da_out[i].astype(BF16), w_gdn_out[i].astype(BF16),
                           w_rw_out[i].astype(BF16), proj, b_gate[i], d_model)
        h = mm_norm_residual(mixed, w_out[i].astype(BF16), h, post_mix_gain[i], lp, lo, hi)

        xn = rmsnorm_cast(h, pre_mlp_gain[i])
        hid = matmul(xn, w_ff1[i].astype(BF16), out_dtype=BF16, relu2=True)
        h = mm_norm_residual(hid, w_ff2[i].astype(BF16), h, post_mlp_gain[i], lp, lo, hi)

    return h.reshape(batch, lp, d_model)[:, hi - seq:hi]
```

```python
import functools
import math

import numpy as np
import jax
import jax.numpy as jnp
from jax import lax
from jax.experimental import pallas as pl
from jax.experimental.pallas import tpu as pltpu

F32 = jnp.float32
BF16 = jnp.bfloat16

CHUNK = 64
N_META = 16
PAD_F = CHUNK - N_META
NORM_EPS = 1e-6
ROPE_THETA = 10000.0

DA_HEADS = 4
DA_HEAD_DIM = 128
DA_V_DIM = 256
GDN_HEADS = 8
GDN_HEAD_DIM = 128
GDN_W = 1024
GDN_CONV = 4
RW_HEADS = 16
RW_HEAD_DIM = 64
RW_W = 1024
RW_DECAY_LORA = 64
RW_A_LORA = 64
RW_GATE_LORA = 160
RW_LORA_P = 384
RW_GN_EPS = RW_HEAD_DIM * 1e-5

C_DA = 0
C_GDN = 3072
C_RW = 7168
C_GATE = 10240
C_GDN_BA = 16384
C_RW_LORA = 16512
IN_COLS_P = 16896

ROW_BLOCK = 256
GROUP_W = 256
SUB = 16
Q_PARTS = 4
GSUM_PIECES = 1
VMEM_LIMIT = 56 * 1024 * 1024


def _pick(n, candidates):
    for c in candidates:
        if n % c == 0:
            return c
    raise ValueError(f"no block size for {n}")


def _cparams(sem):
    return pltpu.CompilerParams(dimension_semantics=sem, vmem_limit_bytes=VMEM_LIMIT)


def _rmsnorm_kernel(x_ref, g_ref, o_ref):
    x = x_ref[...]
    ms = jnp.mean(x * x, axis=-1, keepdims=True)
    o_ref[...] = (x * lax.rsqrt(ms + NORM_EPS) * g_ref[...]).astype(o_ref.dtype)


def rmsnorm_cast(x, gain):
    m, d = x.shape
    tm = _pick(m, (512, 256, 128))
    return pl.pallas_call(
        _rmsnorm_kernel,
        grid=(m // tm,),
        in_specs=[pl.BlockSpec((tm, d), lambda i: (i, 0)),
                  pl.BlockSpec((1, d), lambda i: (0, 0))],
        out_specs=pl.BlockSpec((tm, d), lambda i: (i, 0)),
        out_shape=jax.ShapeDtypeStruct((m, d), BF16),
        compiler_params=_cparams(("parallel",)),
        name="rmsnorm_cast",
    )(x, gain.reshape(1, d).astype(F32))


def _mm_kernel(x_ref, w_ref, o_ref, *, relu2):
    a = jnp.dot(x_ref[...], w_ref[...], preferred_element_type=F32)
    if relu2:
        a = jnp.square(jnp.maximum(a, 0.0))
    o_ref[...] = a.astype(o_ref.dtype)


def matmul(x, w, out_dtype=F32, relu2=False):
    m, k = x.shape
    n = w.shape[1]
    tm = _pick(m, (1280, 768, 512, 256, 128))
    tn = _pick(n, (512, 256, 128))
    return pl.pallas_call(
        functools.partial(_mm_kernel, relu2=relu2),
        grid=(m // tm, n // tn),
        in_specs=[pl.BlockSpec((tm, k), lambda i, j: (i, 0)),
                  pl.BlockSpec((k, tn), lambda i, j: (0, j))],
        out_specs=pl.BlockSpec((tm, tn), lambda i, j: (i, j)),
        out_shape=jax.ShapeDtypeStruct((m, n), out_dtype),
        compiler_params=_cparams(("parallel", "arbitrary")),
        name="matmul",
    )(x, w)


def _mm_norm_res_kernel(x_ref, w_ref, h_ref, g_ref, o_ref, acc_ref, *, nk, tm, blocks_per_seq, lo, hi):
    kk = pl.program_id(1)
    base = (pl.program_id(0) % blocks_per_seq) * tm

    @pl.when(kk == 0)
    def _():
        acc_ref[...] = jnp.zeros_like(acc_ref)

    acc_ref[...] += jnp.dot(x_ref[...], w_ref[...], preferred_element_type=F32)

    @pl.when(kk == nk - 1)
    def _():
        a = acc_ref[...]
        ms = jnp.mean(a * a, axis=-1, keepdims=True)
        y = a * lax.rsqrt(ms + NORM_EPS) * g_ref[...]
        pos = base + lax.broadcasted_iota(jnp.int32, (tm, 1), 0)
        keep = (pos >= lo) & (pos < hi)
        o_ref[...] = jnp.where(keep, h_ref[...] + y, 0.0)


def mm_norm_residual(x, w, h, gain, seq_len_p, lo, hi):
    m, k = x.shape
    n = w.shape[1]
    tm = _pick(seq_len_p, (640, 256, 128))
    tk = _pick(k, (1024, 512))
    nk = k // tk
    return pl.pallas_call(
        functools.partial(_mm_norm_res_kernel, nk=nk, tm=tm, blocks_per_seq=seq_len_p // tm, lo=lo, hi=hi),
        grid=(m // tm, nk),
        in_specs=[pl.BlockSpec((tm, tk), lambda i, kk: (i, kk)),
                  pl.BlockSpec((tk, n), lambda i, kk: (kk, 0)),
                  pl.BlockSpec((tm, n), lambda i, kk: (i, 0)),
                  pl.BlockSpec((1, n), lambda i, kk: (0, 0))],
        out_specs=pl.BlockSpec((tm, n), lambda i, kk: (i, 0)),
        out_shape=jax.ShapeDtypeStruct((m, n), F32),
        scratch_shapes=[pltpu.VMEM((tm, n), F32)],
        compiler_params=_cparams(("parallel", "arbitrary")),
        name="mm_norm_residual",
    )(x, w, h, gain.reshape(1, n).astype(F32))


def _branch_mix_kernel(oa_ref, ob_ref, oc_ref, wa_ref, wb_ref, wc_ref,
                       ga_ref, gb_ref, gc_ref, ba_ref, bb_ref, bc_ref, o_ref):
    acc = None
    for o_r, w_r, g_r, b_r in ((oa_ref, wa_ref, ga_ref, ba_ref),
                               (ob_ref, wb_ref, gb_ref, bb_ref),
                               (oc_ref, wc_ref, gc_ref, bc_ref)):
        y = jnp.dot(o_r[...], w_r[...], preferred_element_type=F32)
        t = jax.nn.sigmoid(g_r[...] + b_r[...]) * y
        acc = t if acc is None else acc + t
    o_ref[...] = acc.astype(o_ref.dtype)


def branch_mix(o_da, o_gdn, o_rw, w_da, w_gdn, w_rw, proj, b_gate, d_model):
    m, k = o_da.shape
    tm = _pick(m, (1280, 768, 512, 256, 128))
    tn = 512
    nj = d_model // tn
    gate_blk0 = C_GATE // tn
    o_spec = pl.BlockSpec((tm, k), lambda i, j: (i, 0))
    w_spec = pl.BlockSpec((k, tn), lambda i, j: (0, j))

    def g_spec(b):
        return pl.BlockSpec((tm, tn), lambda i, j, b=b: (i, gate_blk0 + b * nj + j))

    def b_spec(b):
        return pl.BlockSpec((1, tn), lambda i, j, b=b: (0, b * nj + j))

    bg = b_gate.reshape(1, 3 * d_model).astype(F32)
    return pl.pallas_call(
        _branch_mix_kernel,
        grid=(m // tm, nj),
        in_specs=[o_spec, o_spec, o_spec, w_spec, w_spec, w_spec,
                  g_spec(0), g_spec(1), g_spec(2), b_spec(0), b_spec(1), b_spec(2)],
        out_specs=pl.BlockSpec((tm, tn), lambda i, j: (i, j)),
        out_shape=jax.ShapeDtypeStruct((m, d_model), BF16),
        compiler_params=_cparams(("parallel", "arbitrary")),
        name="branch_mix",
    )(o_da, o_gdn, o_rw, w_da, w_gdn, w_rw, proj, proj, proj, bg, bg, bg)


def _rope_kernel(p_ref, c_ref, s_ref, q_ref, k_ref, v_ref):
    c = c_ref[...]
    s = s_ref[...]
    scale = DA_HEAD_DIM ** -0.5 * math.log2(math.e)
    for g in range(2 * DA_HEADS):
        sl = slice(g * DA_HEAD_DIM, (g + 1) * DA_HEAD_DIM)
        xq = p_ref[:, sl]
        q_ref[:, sl] = ((xq * c + pltpu.roll(xq, DA_HEAD_DIM // 2, 1) * s) * scale).astype(BF16)
        xk = p_ref[:, 1024 + g * DA_HEAD_DIM:1024 + (g + 1) * DA_HEAD_DIM]
        k_ref[:, sl] = (xk * c + pltpu.roll(xk, DA_HEAD_DIM // 2, 1) * s).astype(BF16)
    v_ref[...] = p_ref[:, 2048:3072].astype(BF16)


def rope_prep(proj, cos_t, sin_t, seq_len_p):
    m = proj.shape[0]
    tm = _pick(seq_len_p, (640, 256, 128))
    nb = seq_len_p // tm
    out = jax.ShapeDtypeStruct((m, 1024), BF16)
    o_spec = pl.BlockSpec((tm, 1024), lambda i: (i, 0))
    t_spec = pl.BlockSpec((tm, DA_HEAD_DIM), lambda i: (i % nb, 0))
    return pl.pallas_call(
        _rope_kernel,
        grid=(m // tm,),
        in_specs=[pl.BlockSpec((tm, 3072), lambda i: (i, 0)), t_spec, t_spec],
        out_specs=[o_spec, o_spec, o_spec],
        out_shape=[out, out, out],
        compiler_params=_cparams(("parallel",)),
        name="rope_prep",
    )(proj, cos_t, sin_t)


def _diff_attn_kernel(iq_tab, ik_tab, q_ref, k_ref, v_ref, lam_ref, gain_ref, o_ref,
                      m_ref, l_ref, acc_ref, *, tq, tk, lam_init):
    t = pl.program_id(2)
    iq = iq_tab[t]
    ik = ik_tab[t]
    last = ((iq + 1) * tq - 1) // tk

    @pl.when(ik == 0)
    def _():
        m_ref[...] = jnp.full_like(m_ref, -1e30)
        l_ref[...] = jnp.zeros_like(l_ref)
        acc_ref[...] = jnp.zeros_like(acc_ref)

    def step(masked):
        if masked:
            qpos = iq * tq + lax.broadcasted_iota(jnp.int32, (tq, 1), 0)
            q_end = ((qpos >> 6) + 1) << 6
            kpos = ik * tk + lax.broadcasted_iota(jnp.int32, (1, tk), 1)
            vis = (kpos < q_end) & (kpos >= PAD_F)
        v = v_ref[...]
        rp = tq // Q_PARTS
        ch = [(mp, slice(mp * DA_HEAD_DIM, (mp + 1) * DA_HEAD_DIM), slice(r * rp, (r + 1) * rp))
              for r in range(Q_PARTS) for mp in range(2)]
        n = len(ch)
        s, p, alpha = [None] * n, [None] * n, [None] * n

        def scores(i):
            mp, sl, rs = ch[i]
            x = lax.dot_general(q_ref[rs, sl], k_ref[:, sl], (((1,), (1,)), ((), ())),
                                preferred_element_type=F32)
            s[i] = jnp.where(vis[rs], x, -1e30) if masked else x

        def softmax(i):
            mp, sl, rs = ch[i]
            m_prev = m_ref[mp, rs]
            m_new = jnp.maximum(m_prev, jnp.max(s[i], axis=1, keepdims=True))
            alpha[i] = jnp.exp2(m_prev - m_new)
            p[i] = jnp.exp2(s[i] - m_new)
            l_ref[mp, rs] = alpha[i] * l_ref[mp, rs] + jnp.sum(p[i], axis=1, keepdims=True)
            m_ref[mp, rs] = m_new

        def values(i):
            mp, sl, rs = ch[i]
            acc_ref[mp, rs] = alpha[i] * acc_ref[mp, rs] + jnp.dot(p[i].astype(BF16), v,
                                                                  preferred_element_type=F32)

        for i in range(n + 2):
            if i < n:
                scores(i)
            if 1 <= i <= n:
                softmax(i - 1)
            if i >= 2:
                values(i - 2)

    needs_mask = (ik == 0) | ((ik + 1) * tk > iq * tq + CHUNK)

    @pl.when(needs_mask)
    def _():
        step(True)

    @pl.when(jnp.logical_not(needs_mask))
    def _():
        step(False)

    @pl.when(ik == last)
    def _():
        lv = lam_ref[...]
        lam = (jnp.exp(jnp.sum(lv[0:1] * lv[1:2], axis=1, keepdims=True))
               - jnp.exp(jnp.sum(lv[2:3] * lv[3:4], axis=1, keepdims=True)) + lam_init)
        o = acc_ref[0] / l_ref[0] - lam * (acc_ref[1] / l_ref[1])
        ms = jnp.mean(o * o, axis=-1, keepdims=True)
        o = o * lax.rsqrt(ms + NORM_EPS) * gain_ref[...] * (1.0 - lam_init)
        o_ref[...] = o.astype(o_ref.dtype)


def diff_attention(q, k, v, da_lambda, da_norm_gain, batch, seq_len_p, lam_init):
    tq = _pick(seq_len_p, (640, 256, 128))
    tk = 2 * tq if seq_len_p % (2 * tq) == 0 else tq
    nq = seq_len_p // tq
    nk = seq_len_p // tk
    pairs = [(a, b) for a in range(nq) for b in range(((a + 1) * tq - 1) // tk + 1)]
    iq_tab = jnp.asarray(np.array([p[0] for p in pairs], np.int32))
    ik_tab = jnp.asarray(np.array([p[1] for p in pairs], np.int32))

    def q_map(b, h, t, iq_t, ik_t):
        return (b * nq + iq_t[t], h)

    def kv_map(b, h, t, iq_t, ik_t):
        return (b * nk + ik_t[t], h)

    grid_spec = pltpu.PrefetchScalarGridSpec(
        num_scalar_prefetch=2,
        grid=(batch, DA_HEADS, len(pairs)),
        in_specs=[pl.BlockSpec((tq, DA_V_DIM), q_map),
                  pl.BlockSpec((tk, DA_V_DIM), kv_map),
                  pl.BlockSpec((tk, DA_V_DIM), kv_map),
                  pl.BlockSpec((4, DA_HEAD_DIM), lambda b, h, t, iq_t, ik_t: (0, 0)),
                  pl.BlockSpec((1, DA_V_DIM), lambda b, h, t, iq_t, ik_t: (0, 0))],
        out_specs=pl.BlockSpec((tq, DA_V_DIM), q_map),
        scratch_shapes=[pltpu.VMEM((2, tq, 1), F32), pltpu.VMEM((2, tq, 1), F32),
                        pltpu.VMEM((2, tq, DA_V_DIM), F32)],
    )
    return pl.pallas_call(
        functools.partial(_diff_attn_kernel, tq=tq, tk=tk, lam_init=lam_init),
        grid_spec=grid_spec,
        out_shape=jax.ShapeDtypeStruct((batch * seq_len_p, DA_HEADS * DA_V_DIM), BF16),
        compiler_params=_cparams(("parallel", "parallel", "arbitrary")),
        name="diff_attention",
    )(iq_tab, ik_tab, q, k, v, da_lambda.astype(F32), da_norm_gain.reshape(1, DA_V_DIM).astype(F32))


def _bdot(a, b):
    return jnp.dot(a.astype(BF16), b.astype(BF16), preferred_element_type=F32)


def _bdot_nt(a, b):
    return lax.dot_general(a.astype(BF16), b.astype(BF16), (((1,), (1,)), ((), ())),
                           preferred_element_type=F32)


def _bdot_tn(a, b):
    return lax.dot_general(a.astype(BF16), b.astype(BF16), (((0,), (0,)), ((), ())),
                           preferred_element_type=F32)


def _split_dot(x, w_exact, pieces):
    acc = None
    rem = x
    for _ in range(pieces):
        part = rem.astype(BF16)
        d = jnp.dot(part, w_exact, preferred_element_type=F32)
        acc = d if acc is None else acc + d
        rem = rem - part.astype(F32)
    return acc


def _split_dot_left(w_exact, x, pieces):
    acc = None
    rem = x
    for _ in range(pieces):
        part = rem.astype(BF16)
        d = jnp.dot(w_exact, part, preferred_element_type=F32)
        acc = d if acc is None else acc + d
        rem = rem - part.astype(F32)
    return acc


def _unit_lower_inverses(lows, ii, jj):
    sh = SUB.bit_length() - 1
    sub_bd = (ii >> sh) == (jj >> sh)
    eye = jnp.where(ii == jj, 1.0, 0.0)
    l0s = [jnp.where(sub_bd, low, 0.0) for low in lows]
    xs = [eye - l0 for l0 in l0s]
    pws = [l0.astype(BF16) for l0 in l0s]
    for _ in range(SUB.bit_length() - 2):
        pws = [jnp.dot(pw, pw, preferred_element_type=F32).astype(BF16) for pw in pws]
        xs = [x + jnp.dot(x.astype(BF16), pw, preferred_element_type=F32) for x, pw in zip(xs, pws)]
    s = SUB
    while s < CHUNK:
        sh = s.bit_length() - 1
        off_m = ((ii >> (sh + 1)) == (jj >> (sh + 1))) & ((ii >> sh) > (jj >> sh))
        xbs = [x.astype(BF16) for x in xs]
        ts = [jnp.dot(xb, jnp.where(off_m, low, 0.0).astype(BF16), preferred_element_type=F32).astype(BF16)
              for xb, low in zip(xbs, lows)]
        xs = [x - jnp.dot(t, xb, preferred_element_type=F32) for x, t, xb in zip(xs, ts, xbs)]
        s *= 2
    return xs


def _rows_of_chunk_last(x, tb):
    parts = []
    for c in range(tb // CHUNK):
        last = x[(c + 1) * CHUNK - 1:(c + 1) * CHUNK, :]
        parts.append(jnp.broadcast_to(last, (CHUNK, x.shape[1])))
    return jnp.concatenate(parts, axis=0)


def _shift_rows(x, prev8, s, row8):
    rolled = pltpu.roll(x, s, 0)
    first8 = jnp.where(row8 < s, pltpu.roll(prev8, s, 0), rolled[:8])
    return jnp.concatenate([first8, rolled[8:]], axis=0)


def _gdn_kernel(pq_ref, pk_ref, pv_ref, pz_ref, pba_ref, cq_ref, ck_ref, cv_ref, gv_ref, ng_ref,
                o_ref, s_ref, halo_ref, *, tb):
    grp = pl.program_id(1)

    @pl.when(pl.program_id(2) == 0)
    def _():
        s_ref[...] = jnp.zeros_like(s_ref)
        halo_ref[...] = jnp.zeros_like(halo_ref)

    hd = GDN_HEAD_DIM
    nh = GROUP_W // hd
    ii = lax.broadcasted_iota(jnp.int32, (tb, tb), 0)
    jj = lax.broadcasted_iota(jnp.int32, (tb, tb), 1)
    same = (ii >> 6) == (jj >> 6)
    strict = same & (ii > jj)
    incl = same & (ii >= jj)
    upper = same & (ii <= jj)
    eye = ii == jj
    row8 = lax.broadcasted_iota(jnp.int32, (8, 1), 0)
    gi = lax.broadcasted_iota(jnp.int32, (GROUP_W, GROUP_W), 0)
    gj = lax.broadcasted_iota(jnp.int32, (GROUP_W, GROUP_W), 1)
    head_bd = (gi >> 7) == (gj >> 7)
    ones_bd = jnp.where(head_bd, 1.0, 0.0).astype(BF16)

    def conv_silu(x_ref, idx, cw_ref):
        x = x_ref[...]
        prev8 = halo_ref[idx]
        cw = cw_ref[...]
        acc = x * cw[GDN_CONV - 1:GDN_CONV]
        for s in range(1, GDN_CONV):
            acc = acc + _shift_rows(x, prev8, s, row8) * cw[GDN_CONV - 1 - s:GDN_CONV - s]
        halo_ref[idx] = x[tb - 8:, :]
        return acc * jax.nn.sigmoid(acc)

    def l2n(x):
        return x * lax.rsqrt(_split_dot(x * x, ones_bd, GSUM_PIECES) + 1e-6)

    q = l2n(conv_silu(pq_ref, 0, cq_ref)) * (hd ** -0.5)
    k = l2n(conv_silu(pk_ref, 1, ck_ref))
    v = conv_silu(pv_ref, 2, cv_ref)

    ba = pba_ref[...]
    lane = lax.broadcasted_iota(jnp.int32, (1, 128), 1)
    beta_all = jax.nn.sigmoid(ba)
    xa = ba + gv_ref[1:2, :]
    softplus = jnp.maximum(xa, 0.0) + jnp.log(1.0 + jnp.exp(-jnp.abs(xa)))
    g_all = -jnp.exp(gv_ref[0:1, :]) * softplus

    heads = range(nh)
    qs = [q[:, hh * hd:(hh + 1) * hd] for hh in heads]
    ks = [k[:, hh * hd:(hh + 1) * hd] for hh in heads]
    vs = [v[:, hh * hd:(hh + 1) * hd] for hh in heads]
    k16 = [x.astype(BF16) for x in ks]
    beta = [jnp.sum(jnp.where(lane == grp * nh + hh, beta_all, 0.0), axis=1, keepdims=True) for hh in heads]
    gcol = [jnp.sum(jnp.where(lane == GDN_HEADS + grp * nh + hh, g_all, 0.0), axis=1, keepdims=True)
            for hh in heads]
    gr = [jnp.sum(jnp.where(upper, gcol[hh], 0.0), axis=0, keepdims=True) for hh in heads]
    gc = [jnp.sum(jnp.where(eye, gr[hh], 0.0), axis=1, keepdims=True) for hh in heads]
    dexp = [jnp.exp(jnp.minimum(gc[hh] - gr[hh], 0.0)) for hh in heads]
    tinv = _unit_lower_inverses(
        [jnp.where(strict, beta[hh] * _bdot_nt(k16[hh], k16[hh]) * dexp[hh], 0.0) for hh in heads], ii, jj)
    aqk = [jnp.where(incl, _bdot_nt(qs[hh], k16[hh]) * dexp[hh], 0.0) for hh in heads]
    gamma = [jnp.exp(gc[hh]) for hh in heads]
    wk = [_bdot(tinv[hh], jnp.concatenate([beta[hh] * vs[hh], (beta[hh] * gamma[hh]) * ks[hh]], axis=1))
          for hh in heads]
    aq = [_bdot(aqk[hh], wk[hh]) for hh in heads]
    g_last = [_rows_of_chunk_last(gc[hh], tb) for hh in heads]
    gend = [jnp.exp(g_last[hh]) for hh in heads]
    w0 = jnp.concatenate([wk[hh][:, :hd] for hh in heads], axis=1)
    kw = jnp.concatenate([wk[hh][:, hd:] for hh in heads], axis=1)
    o0 = jnp.concatenate([aq[hh][:, :hd] for hh in heads], axis=1)
    qq = jnp.concatenate([gamma[hh] * qs[hh] - aq[hh][:, hd:] for hh in heads], axis=1)
    kdec = jnp.concatenate([ks[hh] * jnp.exp(g_last[hh] - gc[hh]) for hh in heads], axis=1)
    srow = lax.broadcasted_iota(jnp.int32, (GROUP_W, 1), 0) >> 7

    s = s_ref[...]
    o_chunks = []
    for c in range(tb // CHUNK):
        rs = slice(c * CHUNK, (c + 1) * CHUNK)
        x = _bdot_tn(kdec[rs], jnp.concatenate([kw[rs], w0[rs]], axis=1))
        mcorr = jnp.where(head_bd, x[:, :GROUP_W], 0.0)
        u = jnp.where(head_bd, x[:, GROUP_W:], 0.0)
        o_chunks.append(_bdot(qq[rs], s) + o0[rs])
        decay = gend[0][c * CHUNK:c * CHUNK + 1, :]
        for hh in range(1, nh):
            decay = jnp.where(srow == hh, gend[hh][c * CHUNK:c * CHUNK + 1, :], decay)
        s = decay * s - _bdot(mcorr, s) + u
    s_ref[...] = s

    o = jnp.concatenate(o_chunks, axis=0)
    ms = _split_dot(o * o, ones_bd, GSUM_PIECES) * (1.0 / hd)
    z = pz_ref[...]
    o = o * lax.rsqrt(ms + NORM_EPS) * ng_ref[...] * (z * jax.nn.sigmoid(z))
    o_ref[...] = o.astype(o_ref.dtype)


def gdn_branch(proj, conv_w, a_log, dt_bias, norm_gain, batch, seq_len_p):
    tb = ROW_BLOCK
    nb = seq_len_p // tb
    ng = GDN_W // GROUP_W
    c0 = C_GDN // GROUP_W

    def p_spec(off):
        return pl.BlockSpec((tb, GROUP_W), lambda b, g, i, off=off: (b * nb + i, c0 + off * ng + g))

    def c_spec(off):
        return pl.BlockSpec((GDN_CONV, GROUP_W), lambda b, g, i, off=off: (0, off * ng + g))

    gvec = jnp.zeros((8, 128), F32)
    gvec = gvec.at[0, GDN_HEADS:2 * GDN_HEADS].set(a_log.astype(F32))
    gvec = gvec.at[1, GDN_HEADS:2 * GDN_HEADS].set(dt_bias.astype(F32))
    ngain = jnp.tile(norm_gain.astype(F32), GROUP_W // GDN_HEAD_DIM).reshape(1, GROUP_W)
    cw = conv_w.astype(F32)
    return pl.pallas_call(
        functools.partial(_gdn_kernel, tb=tb),
        grid=(batch, ng, nb),
        in_specs=[p_spec(0), p_spec(1), p_spec(2), p_spec(3),
                  pl.BlockSpec((tb, 128), lambda b, g, i: (b * nb + i, C_GDN_BA // 128)),
                  c_spec(0), c_spec(1), c_spec(2),
                  pl.BlockSpec((8, 128), lambda b, g, i: (0, 0)),
                  pl.BlockSpec((1, GROUP_W), lambda b, g, i: (0, 0))],
        out_specs=pl.BlockSpec((tb, GROUP_W), lambda b, g, i: (b * nb + i, g)),
        out_shape=jax.ShapeDtypeStruct((batch * seq_len_p, GDN_W), BF16),
        scratch_shapes=[pltpu.VMEM((GROUP_W, GROUP_W), F32), pltpu.VMEM((3, 8, GROUP_W), F32)],
        compiler_params=_cparams(("parallel", "parallel", "arbitrary")),
        name="gdn_branch",
    )(proj, proj, proj, proj, proj, cw, cw, cw, gvec, ngain)


def _rwkv_kernel(pr_ref, pk_ref, pv_ref, plo_ref, mur_ref, muk_ref, muv_ref, mul_ref, vec_ref,
                 wup_ref, aup_ref, gup_ref, o_ref, s_ref, prev_ref, prevl_ref, *, tb):
    @pl.when(pl.program_id(2) == 0)
    def _():
        s_ref[...] = jnp.zeros_like(s_ref)
        prev_ref[...] = jnp.zeros_like(prev_ref)
        prevl_ref[...] = jnp.zeros_like(prevl_ref)

    hd = RW_HEAD_DIM
    nh = GROUP_W // hd
    ii = lax.broadcasted_iota(jnp.int32, (tb, tb), 0)
    jj = lax.broadcasted_iota(jnp.int32, (tb, tb), 1)
    same = (ii >> 6) == (jj >> 6)
    strict = same & (ii > jj)
    incl = same & (ii >= jj)
    tri = jnp.where(incl, 1.0, 0.0).astype(BF16)
    gi = lax.broadcasted_iota(jnp.int32, (GROUP_W, GROUP_W), 0)
    gj = lax.broadcasted_iota(jnp.int32, (GROUP_W, GROUP_W), 1)
    head_bd = (gi >> 6) == (gj >> 6)
    ones_bd = jnp.where(head_bd, 1.0, 0.0).astype(BF16)
    lane_head = lax.broadcasted_iota(jnp.int32, (1, GROUP_W), 1) >> 6
    row0 = lax.broadcasted_iota(jnp.int32, (tb, 1), 0) == 0

    def lerp_shift(x_ref, prev, mu_ref):
        x = x_ref[...]
        xs = jnp.where(row0, prev[7:8, :], pltpu.roll(x, 1, 0))
        return x + (xs - x) * mu_ref[...], x[tb - 8:, :]

    r, prev_r = lerp_shift(pr_ref, prev_ref[0], mur_ref)
    kx, prev_k = lerp_shift(pk_ref, prev_ref[1], muk_ref)
    v, prev_v = lerp_shift(pv_ref, prev_ref[2], muv_ref)
    lo, prev_l = lerp_shift(plo_ref, prevl_ref[...], mul_ref)
    prev_ref[0] = prev_r
    prev_ref[1] = prev_k
    prev_ref[2] = prev_v
    prevl_ref[...] = prev_l

    w0, a0, k_k, k_a = vec_ref[0:1, :], vec_ref[1:2, :], vec_ref[2:3, :], vec_ref[3:4, :]
    r_k, gn_g, gn_b = vec_ref[4:5, :], vec_ref[5:6, :], vec_ref[6:7, :]
    zdec = w0 + _bdot(jnp.tanh(lo[:, :RW_DECAY_LORA]), wup_ref[...])
    lw = jax.nn.sigmoid(zdec) * (-math.exp(-0.5))
    a = jax.nn.sigmoid(a0 + _bdot(lo[:, RW_DECAY_LORA:RW_DECAY_LORA + RW_A_LORA], aup_ref[...]))
    gate = _bdot(jax.nn.sigmoid(lo[:, RW_DECAY_LORA + RW_A_LORA:]), gup_ref[...])
    kkx = kx * k_k
    kk = kkx * lax.rsqrt(_split_dot(kkx * kkx, ones_bd, GSUM_PIECES) + 1e-6)
    k2 = kx * (1.0 + (a - 1.0) * k_a)
    b = kk * a

    lwc = _split_dot_left(tri, lw, 3)
    w_inv = jnp.exp(-lwc)
    rt = r * jnp.exp(lwc)
    kt = k2 * w_inv
    bt = b * w_inv
    kap = kk * jnp.exp(lwc - lw)
    l_last = _rows_of_chunk_last(lwc, tb)
    dec = jnp.exp(l_last - lwc)
    kw_end = k2 * dec
    bw_end = b * dec
    w_end = jnp.exp(l_last)

    heads = range(nh)
    kt16, bt16 = kt.astype(BF16), bt.astype(BF16)
    kap_m = [jnp.where(lane_head == h, kap, 0.0).astype(BF16) for h in heads]
    rt_m = [jnp.where(lane_head == h, rt, 0.0).astype(BF16) for h in heads]
    v_m = [jnp.where(lane_head == h, v, 0.0).astype(BF16) for h in heads]
    tinv = _unit_lower_inverses([jnp.where(strict, _bdot_nt(kap_m[h], bt16), 0.0) for h in heads], ii, jj)
    tinv = [t.astype(BF16) for t in tinv]
    akk_v = [_bdot(jnp.where(strict, _bdot_nt(kap_m[h], kt16), 0.0), v_m[h]) for h in heads]
    a_rk = [jnp.where(incl, _bdot_nt(rt_m[h], kt16), 0.0).astype(BF16) for h in heads]
    a_rb = [jnp.where(incl, _bdot_nt(rt_m[h], bt16), 0.0).astype(BF16) for h in heads]
    kp_h = [_bdot(tinv[h], kap_m[h]) for h in heads]
    p0_h = [_bdot(tinv[h], akk_v[h]) for h in heads]
    rq_h = [_bdot(a_rb[h], kp_h[h]) for h in heads]
    o0_h = [_bdot(a_rk[h], v_m[h]) - _bdot(a_rb[h], p0_h[h]) for h in heads]
    kp = sum(kp_h[1:], kp_h[0])
    p0 = sum(p0_h[1:], p0_h[0])
    rq = rt - sum(rq_h[1:], rq_h[0])
    o0 = sum(o0_h[1:], o0_h[0])

    s = s_ref[...]
    o_chunks = []
    for c in range(tb // CHUNK):
        rs = slice(c * CHUNK, (c + 1) * CHUNK)
        mcorr = jnp.where(head_bd, _bdot_tn(kp[rs], bw_end[rs]), 0.0)
        u = jnp.where(head_bd, _bdot_tn(jnp.concatenate([v[rs], p0[rs]], axis=0),
                                        jnp.concatenate([kw_end[rs], -bw_end[rs]], axis=0)), 0.0)
        o_chunks.append(_bdot_nt(rq[rs], s) + o0[rs])
        s = s * w_end[c * CHUNK:c * CHUNK + 1, :] - _bdot(s, mcorr) + u
    s_ref[...] = s

    o = jnp.concatenate(o_chunks, axis=0)
    mean = _split_dot(o, ones_bd, GSUM_PIECES) * (1.0 / hd)
    d = o - mean
    var = _split_dot(d * d, ones_bd, GSUM_PIECES) * (1.0 / hd)
    o = d * lax.rsqrt(var + RW_GN_EPS) * gn_g + gn_b
    o = o + _split_dot(r * k2 * r_k, ones_bd, GSUM_PIECES) * v
    o_ref[...] = (o * gate).astype(o_ref.dtype)


def rwkv_branch(proj, mu, w0, w_up, a0, a_up, g_up, k_k, k_a, r_k, gn_gain, gn_bias, batch, seq_len_p):
    tb = ROW_BLOCK
    nb = seq_len_p // tb
    ng = RW_W // GROUP_W
    c0 = C_RW // GROUP_W
    n_lora = RW_DECAY_LORA + RW_A_LORA + RW_GATE_LORA

    def p_spec(off):
        return pl.BlockSpec((tb, GROUP_W), lambda b, g, i, off=off: (b * nb + i, c0 + off * ng + g))

    def mu_spec(off):
        return pl.BlockSpec((1, GROUP_W), lambda b, g, i, off=off: (0, off * ng + g))

    def w_spec(rows):
        return pl.BlockSpec((rows, GROUP_W), lambda b, g, i: (0, g))

    mu = mu.astype(F32)
    mu_rkv = mu[:3 * RW_W].reshape(1, 3 * RW_W)
    mu_l = jnp.pad(mu[3 * RW_W:], (0, RW_LORA_P - n_lora)).reshape(1, RW_LORA_P)
    vec = jnp.stack([w0, a0, k_k, k_a, r_k, gn_gain, gn_bias, jnp.zeros_like(w0)]).astype(F32)
    g_up_p = jnp.pad(g_up, ((0, RW_LORA_P - RW_DECAY_LORA - RW_A_LORA - RW_GATE_LORA), (0, 0))).astype(BF16)
    return pl.pallas_call(
        functools.partial(_rwkv_kernel, tb=tb),
        grid=(batch, ng, nb),
        in_specs=[p_spec(0), p_spec(1), p_spec(2),
                  pl.BlockSpec((tb, RW_LORA_P), lambda b, g, i: (b * nb + i, C_RW_LORA // RW_LORA_P)),
                  mu_spec(0), mu_spec(1), mu_spec(2),
                  pl.BlockSpec((1, RW_LORA_P), lambda b, g, i: (0, 0)),
                  w_spec(8), w_spec(RW_DECAY_LORA), w_spec(RW_A_LORA),
                  w_spec(RW_LORA_P - RW_DECAY_LORA - RW_A_LORA)],
        out_specs=pl.BlockSpec((tb, GROUP_W), lambda b, g, i: (b * nb + i, g)),
        out_shape=jax.ShapeDtypeStruct((batch * seq_len_p, RW_W), BF16),
        scratch_shapes=[pltpu.VMEM((GROUP_W, GROUP_W), F32), pltpu.VMEM((3, 8, GROUP_W), F32),
                        pltpu.VMEM((8, RW_LORA_P), F32)],
        compiler_params=_cparams(("parallel", "parallel", "arbitrary")),
        name="rwkv_branch",
    )(proj, proj, proj, proj, mu_rkv, mu_rkv, mu_rkv, mu_l, vec,
      w_up.astype(BF16), a_up.astype(BF16), g_up_p)


def _pad_in_weights(w_in_i):
    d = w_in_i.shape[0]
    gdn_cols = 4 * GDN_W + 2 * GDN_HEADS
    rw_cols = 3 * RW_W + RW_DECAY_LORA + RW_A_LORA + RW_GATE_LORA
    o1 = 3072
    o2 = o1 + gdn_cols
    o3 = o2 + rw_cols

    def z(n):
        return jnp.zeros((d, n), w_in_i.dtype)

    parts = [w_in_i[:, :o1],
             w_in_i[:, o1:o1 + 4 * GDN_W],
             w_in_i[:, o2:o2 + 3 * RW_W],
             w_in_i[:, o3:],
             w_in_i[:, o1 + 4 * GDN_W:o2], z(128 - 2 * GDN_HEADS),
             w_in_i[:, o2 + 3 * RW_W:o3], z(RW_LORA_P - (rw_cols - 3 * RW_W))]
    return jnp.concatenate(parts, axis=1).astype(BF16)


def kernel(x, meta_tokens, pre_mix_gain, post_mix_gain, pre_mlp_gain, post_mlp_gain, w_in, b_gate, da_lambda, da_norm_gain, gdn_conv, gdn_a_log, gdn_dt_bias, gdn_norm_gain, rw_mu, rw_w0, rw_w_up, rw_a0, rw_a_up, rw_g_up, rw_k_k, rw_k_a, rw_r_k, rw_gn_gain, rw_gn_bias, w_da_out, w_gdn_out, w_rw_out, w_out, w_ff1, w_ff2):
    batch, seq, d_model = x.shape
    depth = w_in.shape[0]
    n_tok = N_META + seq
    lo, hi = PAD_F, PAD_F + n_tok
    lp = -(-hi // 1280) * 1280 if hi > 1280 else -(-hi // ROW_BLOCK) * ROW_BLOCK
    m = batch * lp

    meta = jnp.broadcast_to(meta_tokens[None].astype(x.dtype), (batch, N_META, d_model))
    h = jnp.concatenate([jnp.zeros((batch, PAD_F, d_model), x.dtype), meta, x,
                         jnp.zeros((batch, lp - hi, d_model), x.dtype)], axis=1).reshape(m, d_model)

    inv = 1.0 / (ROPE_THETA ** (jnp.arange(0, DA_HEAD_DIM, 2, dtype=F32) / DA_HEAD_DIM))
    ang = (jnp.arange(lp, dtype=F32) - PAD_F)[:, None] * inv[None, :]
    cos_t = jnp.concatenate([jnp.cos(ang), jnp.cos(ang)], axis=1)
    sin_t = jnp.concatenate([-jnp.sin(ang), jnp.sin(ang)], axis=1)

    for i in range(depth):
        lam_init = 0.8 - 0.6 * math.exp(-0.3 * i)
        xn = rmsnorm_cast(h, pre_mix_gain[i])
        proj = matmul(xn, _pad_in_weights(w_in[i]))

        q, k, v = rope_prep(proj, cos_t, sin_t, lp)
        o_da = diff_attention(q, k, v, da_lambda[i], da_norm_gain[i], batch, lp, lam_init)
        o_gdn = gdn_branch(proj, gdn_conv[i], gdn_a_log[i], gdn_dt_bias[i], gdn_norm_gain[i], batch, lp)
        o_rw = rwkv_branch(proj, rw_mu[i], rw_w0[i], rw_w_up[i], rw_a0[i], rw_a_up[i], rw_g_up[i],
                           rw_k_k[i], rw_k_a[i], rw_r_k[i], rw_gn_gain[i], rw_gn_bias[i], batch, lp)

        mixed = branch_mix(o_da, o_gdn, o_rw, w_da_out[i].astype(BF16), w_gdn_out[i].astype(BF16),
                           w_rw_out[i].astype(BF16), proj, b_gate[i], d_model)
        h = mm_norm_residual(mixed, w_out[i].astype(BF16), h, post_mix_gain[i], lp, lo, hi)

        xn = rmsnorm_cast(h, pre_mlp_gain[i])
        hid = matmul(xn, w_ff1[i].astype(BF16), out_dtype=BF16, relu2=True)
        h = mm_norm_residual(hid, w_ff2[i].astype(BF16), h, post_mlp_gain[i], lp, lo, hi)

    return h.reshape(batch, lp, d_model)[:, hi - seq:hi]
```

```python
import functools
import math

import numpy as np
import jax
import jax.numpy as jnp
from jax import lax
from jax.experimental import pallas as pl
from jax.experimental.pallas import tpu as pltpu

F32 = jnp.float32
BF16 = jnp.bfloat16

CHUNK = 64
N_META = 16
PAD_F = CHUNK - N_META
NORM_EPS = 1e-6
ROPE_THETA = 10000.0

DA_HEADS = 4
DA_HEAD_DIM = 128
DA_V_DIM = 256
GDN_HEADS = 8
GDN_HEAD_DIM = 128
GDN_W = 1024
GDN_CONV = 4
RW_HEADS = 16
RW_HEAD_DIM = 64
RW_W = 1024
RW_DECAY_LORA = 64
RW_A_LORA = 64
RW_GATE_LORA = 160
RW_LORA_P = 384
RW_GN_EPS = RW_HEAD_DIM * 1e-5

C_DA = 0
C_GDN = 3072
C_RW = 7168
C_GATE = 10240
C_GDN_BA = 16384
C_RW_LORA = 16512
IN_COLS_P = 16896

ROW_BLOCK = 256
GROUP_W = 256
RW_GROUPS_PER_STEP = 2
GDN_GROUPS_PER_STEP = 4
SUB = 16
Q_PARTS = 4
GSUM_PIECES = 1
VMEM_LIMIT = 56 * 1024 * 1024


def _pick(n, candidates):
    for c in candidates:
        if n % c == 0:
            return c
    raise ValueError(f"no block size for {n}")


def _cparams(sem):
    return pltpu.CompilerParams(dimension_semantics=sem, vmem_limit_bytes=VMEM_LIMIT)


def _rmsnorm_kernel(x_ref, g_ref, o_ref):
    x = x_ref[...]
    ms = jnp.mean(x * x, axis=-1, keepdims=True)
    o_ref[...] = (x * lax.rsqrt(ms + NORM_EPS) * g_ref[...]).astype(o_ref.dtype)


def rmsnorm_cast(x, gain):
    m, d = x.shape
    tm = _pick(m, (512, 256, 128))
    return pl.pallas_call(
        _rmsnorm_kernel,
        grid=(m // tm,),
        in_specs=[pl.BlockSpec((tm, d), lambda i: (i, 0)),
                  pl.BlockSpec((1, d), lambda i: (0, 0))],
        out_specs=pl.BlockSpec((tm, d), lambda i: (i, 0)),
        out_shape=jax.ShapeDtypeStruct((m, d), BF16),
        compiler_params=_cparams(("parallel",)),
        name="rmsnorm_cast",
    )(x, gain.reshape(1, d).astype(F32))


def _mm_kernel(x_ref, w_ref, o_ref, *, relu2):
    a = jnp.dot(x_ref[...], w_ref[...], preferred_element_type=F32)
    if relu2:
        a = jnp.square(jnp.maximum(a, 0.0))
    o_ref[...] = a.astype(o_ref.dtype)


def matmul(x, w, out_dtype=F32, relu2=False):
    m, k = x.shape
    n = w.shape[1]
    tm = _pick(m, (1280, 768, 512, 256, 128))
    tn = _pick(n, (512, 256, 128))
    return pl.pallas_call(
        functools.partial(_mm_kernel, relu2=relu2),
        grid=(m // tm, n // tn),
        in_specs=[pl.BlockSpec((tm, k), lambda i, j: (i, 0)),
                  pl.BlockSpec((k, tn), lambda i, j: (0, j))],
        out_specs=pl.BlockSpec((tm, tn), lambda i, j: (i, j)),
        out_shape=jax.ShapeDtypeStruct((m, n), out_dtype),
        compiler_params=_cparams(("parallel", "arbitrary")),
        name="matmul",
    )(x, w)


def _mm_norm_res_kernel(x_ref, w_ref, h_ref, g_ref, g2_ref, o_ref, o2_ref, acc_ref, *,
                        nk, tm, blocks_per_seq, lo, hi):
    kk = pl.program_id(1)
    base = (pl.program_id(0) % blocks_per_seq) * tm

    @pl.when(kk == 0)
    def _():
        acc_ref[...] = jnp.zeros_like(acc_ref)

    acc_ref[...] += jnp.dot(x_ref[...], w_ref[...], preferred_element_type=F32)

    @pl.when(kk == nk - 1)
    def _():
        a = acc_ref[...]
        ms = jnp.mean(a * a, axis=-1, keepdims=True)
        y = a * lax.rsqrt(ms + NORM_EPS) * g_ref[...]
        pos = base + lax.broadcasted_iota(jnp.int32, (tm, 1), 0)
        keep = (pos >= lo) & (pos < hi)
        hn = jnp.where(keep, h_ref[...] + y, 0.0)
        o_ref[...] = hn
        ms2 = jnp.mean(hn * hn, axis=-1, keepdims=True)
        o2_ref[...] = (hn * lax.rsqrt(ms2 + NORM_EPS) * g2_ref[...]).astype(o2_ref.dtype)


def mm_norm_residual(x, w, h, gain, next_gain, seq_len_p, lo, hi):
    m, k = x.shape
    n = w.shape[1]
    tm = _pick(seq_len_p, (640, 256, 128))
    tk = _pick(k, (1024, 512))
    nk = k // tk
    row_spec = pl.BlockSpec((tm, n), lambda i, kk: (i, 0))
    vec_spec = pl.BlockSpec((1, n), lambda i, kk: (0, 0))
    return pl.pallas_call(
        functools.partial(_mm_norm_res_kernel, nk=nk, tm=tm, blocks_per_seq=seq_len_p // tm, lo=lo, hi=hi),
        grid=(m // tm, nk),
        in_specs=[pl.BlockSpec((tm, tk), lambda i, kk: (i, kk)),
                  pl.BlockSpec((tk, n), lambda i, kk: (kk, 0)),
                  row_spec, vec_spec, vec_spec],
        out_specs=[row_spec, row_spec],
        out_shape=[jax.ShapeDtypeStruct((m, n), F32), jax.ShapeDtypeStruct((m, n), BF16)],
        scratch_shapes=[pltpu.VMEM((tm, n), F32)],
        compiler_params=_cparams(("parallel", "arbitrary")),
        name="mm_norm_residual",
    )(x, w, h, gain.reshape(1, n).astype(F32), next_gain.reshape(1, n).astype(F32))


def _branch_mix_kernel(oa_ref, ob_ref, oc_ref, wa_ref, wb_ref, wc_ref,
                       ga_ref, gb_ref, gc_ref, ba_ref, bb_ref, bc_ref, o_ref):
    acc = None
    for o_r, w_r, g_r, b_r in ((oa_ref, wa_ref, ga_ref, ba_ref),
                               (ob_ref, wb_ref, gb_ref, bb_ref),
                               (oc_ref, wc_ref, gc_ref, bc_ref)):
        y = jnp.dot(o_r[...], w_r[...], preferred_element_type=F32)
        t = jax.nn.sigmoid(g_r[...] + b_r[...]) * y
        acc = t if acc is None else acc + t
    o_ref[...] = acc.astype(o_ref.dtype)


def branch_mix(o_da, o_gdn, o_rw, w_da, w_gdn, w_rw, proj, b_gate, d_model):
    m, k = o_da.shape
    tm = _pick(m, (1280, 768, 512, 256, 128))
    tn = 512
    nj = d_model // tn
    gate_blk0 = C_GATE // tn
    o_spec = pl.BlockSpec((tm, k), lambda i, j: (i, 0))
    w_spec = pl.BlockSpec((k, tn), lambda i, j: (0, j))

    def g_spec(b):
        return pl.BlockSpec((tm, tn), lambda i, j, b=b: (i, gate_blk0 + b * nj + j))

    def b_spec(b):
        return pl.BlockSpec((1, tn), lambda i, j, b=b: (0, b * nj + j))

    bg = b_gate.reshape(1, 3 * d_model).astype(F32)
    return pl.pallas_call(
        _branch_mix_kernel,
        grid=(m // tm, nj),
        in_specs=[o_spec, o_spec, o_spec, w_spec, w_spec, w_spec,
                  g_spec(0), g_spec(1), g_spec(2), b_spec(0), b_spec(1), b_spec(2)],
        out_specs=pl.BlockSpec((tm, tn), lambda i, j: (i, j)),
        out_shape=jax.ShapeDtypeStruct((m, d_model), BF16),
        compiler_params=_cparams(("parallel", "arbitrary")),
        name="branch_mix",
    )(o_da, o_gdn, o_rw, w_da, w_gdn, w_rw, proj, proj, proj, bg, bg, bg)


def _rope_kernel(p_ref, c_ref, s_ref, q_ref, k_ref, v_ref):
    c = c_ref[...]
    s = s_ref[...]
    scale = DA_HEAD_DIM ** -0.5 * math.log2(math.e)
    for g in range(2 * DA_HEADS):
        sl = slice(g * DA_HEAD_DIM, (g + 1) * DA_HEAD_DIM)
        xq = p_ref[:, sl]
        q_ref[:, sl] = ((xq * c + pltpu.roll(xq, DA_HEAD_DIM // 2, 1) * s) * scale).astype(BF16)
        xk = p_ref[:, 1024 + g * DA_HEAD_DIM:1024 + (g + 1) * DA_HEAD_DIM]
        k_ref[:, sl] = (xk * c + pltpu.roll(xk, DA_HEAD_DIM // 2, 1) * s).astype(BF16)
    v_ref[...] = p_ref[:, 2048:3072].astype(BF16)


def rope_prep(proj, cos_t, sin_t, seq_len_p):
    m = proj.shape[0]
    tm = _pick(seq_len_p, (640, 256, 128))
    nb = seq_len_p // tm
    out = jax.ShapeDtypeStruct((m, 1024), BF16)
    o_spec = pl.BlockSpec((tm, 1024), lambda i: (i, 0))
    t_spec = pl.BlockSpec((tm, DA_HEAD_DIM), lambda i: (i % nb, 0))
    return pl.pallas_call(
        _rope_kernel,
        grid=(m // tm,),
        in_specs=[pl.BlockSpec((tm, 3072), lambda i: (i, 0)), t_spec, t_spec],
        out_specs=[o_spec, o_spec, o_spec],
        out_shape=[out, out, out],
        compiler_params=_cparams(("parallel",)),
        name="rope_prep",
    )(proj, cos_t, sin_t)


def _diff_attn_kernel(iq_tab, ik_tab, q_ref, k_ref, v_ref, lam_ref, gain_ref, o_ref,
                      m_ref, l_ref, acc_ref, *, tq, tk, lam_init):
    t = pl.program_id(2)
    iq = iq_tab[t]
    ik = ik_tab[t]
    last = ((iq + 1) * tq - 1) // tk

    @pl.when(ik == 0)
    def _():
        m_ref[...] = jnp.full_like(m_ref, -1e30)
        l_ref[...] = jnp.zeros_like(l_ref)
        acc_ref[...] = jnp.zeros_like(acc_ref)

    def step(masked):
        if masked:
            qpos = iq * tq + lax.broadcasted_iota(jnp.int32, (tq, 1), 0)
            q_end = ((qpos >> 6) + 1) << 6
            kpos = ik * tk + lax.broadcasted_iota(jnp.int32, (1, tk), 1)
            vis = (kpos < q_end) & (kpos >= PAD_F)
        v = v_ref[...]
        rp = tq // Q_PARTS
        ch = [(mp, slice(mp * DA_HEAD_DIM, (mp + 1) * DA_HEAD_DIM), slice(r * rp, (r + 1) * rp))
              for r in range(Q_PARTS) for mp in range(2)]
        n = len(ch)
        s, p, alpha = [None] * n, [None] * n, [None] * n

        def scores(i):
            mp, sl, rs = ch[i]
            x = lax.dot_general(q_ref[rs, sl], k_ref[:, sl], (((1,), (1,)), ((), ())),
                                preferred_element_type=F32)
            s[i] = jnp.where(vis[rs], x, -1e30) if masked else x

        def softmax(i):
            mp, sl, rs = ch[i]
            m_prev = m_ref[mp, rs]
            m_new = jnp.maximum(m_prev, jnp.max(s[i], axis=1, keepdims=True))
            alpha[i] = jnp.exp2(m_prev - m_new)
            p[i] = jnp.exp2(s[i] - m_new)
            l_ref[mp, rs] = alpha[i] * l_ref[mp, rs] + jnp.sum(p[i], axis=1, keepdims=True)
            m_ref[mp, rs] = m_new

        def values(i):
            mp, sl, rs = ch[i]
            acc_ref[mp, rs] = alpha[i] * acc_ref[mp, rs] + jnp.dot(p[i].astype(BF16), v,
                                                                  preferred_element_type=F32)

        for i in range(n + 2):
            if i < n:
                scores(i)
            if 1 <= i <= n:
                softmax(i - 1)
            if i >= 2:
                values(i - 2)

    needs_mask = (ik == 0) | ((ik + 1) * tk > iq * tq + CHUNK)

    @pl.when(needs_mask)
    def _():
        step(True)

    @pl.when(jnp.logical_not(needs_mask))
    def _():
        step(False)

    @pl.when(ik == last)
    def _():
        lv = lam_ref[...]
        lam = (jnp.exp(jnp.sum(lv[0:1] * lv[1:2], axis=1, keepdims=True))
               - jnp.exp(jnp.sum(lv[2:3] * lv[3:4], axis=1, keepdims=True)) + lam_init)
        o = acc_ref[0] / l_ref[0] - lam * (acc_ref[1] / l_ref[1])
        ms = jnp.mean(o * o, axis=-1, keepdims=True)
        o = o * lax.rsqrt(ms + NORM_EPS) * gain_ref[...] * (1.0 - lam_init)
        o_ref[...] = o.astype(o_ref.dtype)


def diff_attention(q, k, v, da_lambda, da_norm_gain, batch, seq_len_p, lam_init):
    tq = _pick(seq_len_p, (640, 256, 128))
    tk = 2 * tq if seq_len_p % (2 * tq) == 0 else tq
    nq = seq_len_p // tq
    nk = seq_len_p // tk
    pairs = [(a, b) for a in range(nq) for b in range(((a + 1) * tq - 1) // tk + 1)]
    iq_tab = jnp.asarray(np.array([p[0] for p in pairs], np.int32))
    ik_tab = jnp.asarray(np.array([p[1] for p in pairs], np.int32))

    def q_map(b, h, t, iq_t, ik_t):
        return (b * nq + iq_t[t], h)

    def kv_map(b, h, t, iq_t, ik_t):
        return (b * nk + ik_t[t], h)

    grid_spec = pltpu.PrefetchScalarGridSpec(
        num_scalar_prefetch=2,
        grid=(batch, DA_HEADS, len(pairs)),
        in_specs=[pl.BlockSpec((tq, DA_V_DIM), q_map),
                  pl.BlockSpec((tk, DA_V_DIM), kv_map),
                  pl.BlockSpec((tk, DA_V_DIM), kv_map),
                  pl.BlockSpec((4, DA_HEAD_DIM), lambda b, h, t, iq_t, ik_t: (0, 0)),
                  pl.BlockSpec((1, DA_V_DIM), lambda b, h, t, iq_t, ik_t: (0, 0))],
        out_specs=pl.BlockSpec((tq, DA_V_DIM), q_map),
        scratch_shapes=[pltpu.VMEM((2, tq, 1), F32), pltpu.VMEM((2, tq, 1), F32),
                        pltpu.VMEM((2, tq, DA_V_DIM), F32)],
    )
    return pl.pallas_call(
        functools.partial(_diff_attn_kernel, tq=tq, tk=tk, lam_init=lam_init),
        grid_spec=grid_spec,
        out_shape=jax.ShapeDtypeStruct((batch * seq_len_p, DA_HEADS * DA_V_DIM), BF16),
        compiler_params=_cparams(("parallel", "parallel", "arbitrary")),
        name="diff_attention",
    )(iq_tab, ik_tab, q, k, v, da_lambda.astype(F32), da_norm_gain.reshape(1, DA_V_DIM).astype(F32))


def _bdot(a, b):
    return jnp.dot(a.astype(BF16), b.astype(BF16), preferred_element_type=F32)


def _bdot_nt(a, b):
    return lax.dot_general(a.astype(BF16), b.astype(BF16), (((1,), (1,)), ((), ())),
                           preferred_element_type=F32)


def _bdot_tn(a, b):
    return lax.dot_general(a.astype(BF16), b.astype(BF16), (((0,), (0,)), ((), ())),
                           preferred_element_type=F32)


def _split_dot(x, w_exact, pieces):
    acc = None
    rem = x
    for _ in range(pieces):
        part = rem.astype(BF16)
        d = jnp.dot(part, w_exact, preferred_element_type=F32)
        acc = d if acc is None else acc + d
        rem = rem - part.astype(F32)
    return acc


def _split_dot_left(w_exact, x, pieces):
    acc = None
    rem = x
    for _ in range(pieces):
        part = rem.astype(BF16)
        d = jnp.dot(w_exact, part, preferred_element_type=F32)
        acc = d if acc is None else acc + d
        rem = rem - part.astype(F32)
    return acc


def _unit_lower_inverses(lows, ii, jj):
    sh = SUB.bit_length() - 1
    sub_bd = (ii >> sh) == (jj >> sh)
    eye = jnp.where(ii == jj, 1.0, 0.0)
    l0s = [jnp.where(sub_bd, low, 0.0) for low in lows]
    xs = [eye - l0 for l0 in l0s]
    pws = [l0.astype(BF16) for l0 in l0s]
    for _ in range(SUB.bit_length() - 2):
        pws = [jnp.dot(pw, pw, preferred_element_type=F32).astype(BF16) for pw in pws]
        xs = [x + jnp.dot(x.astype(BF16), pw, preferred_element_type=F32) for x, pw in zip(xs, pws)]
    s = SUB
    while s < CHUNK:
        sh = s.bit_length() - 1
        off_m = ((ii >> (sh + 1)) == (jj >> (sh + 1))) & ((ii >> sh) > (jj >> sh))
        xbs = [x.astype(BF16) for x in xs]
        ts = [jnp.dot(xb, jnp.where(off_m, low, 0.0).astype(BF16), preferred_element_type=F32).astype(BF16)
              for xb, low in zip(xbs, lows)]
        xs = [x - jnp.dot(t, xb, preferred_element_type=F32) for x, t, xb in zip(xs, ts, xbs)]
        s *= 2
    return xs


def _rows_of_chunk_last(x, tb):
    parts = []
    for c in range(tb // CHUNK):
        last = x[(c + 1) * CHUNK - 1:(c + 1) * CHUNK, :]
        parts.append(jnp.broadcast_to(last, (CHUNK, x.shape[1])))
    return jnp.concatenate(parts, axis=0)


def _shift_rows(x, prev8, s, row8):
    rolled = pltpu.roll(x, s, 0)
    first8 = jnp.where(row8 < s, pltpu.roll(prev8, s, 0), rolled[:8])
    return jnp.concatenate([first8, rolled[8:]], axis=0)


def _gdn_kernel(pq_ref, pk_ref, pv_ref, pz_ref, pba_ref, cq_ref, ck_ref, cv_ref, gv_ref, ng_ref,
                o_ref, s_ref, halo_ref, *, tb):
    grp = pl.program_id(1)

    @pl.when(pl.program_id(2) == 0)
    def _():
        s_ref[...] = jnp.zeros_like(s_ref)
        halo_ref[...] = jnp.zeros_like(halo_ref)

    hd = GDN_HEAD_DIM
    nh = GROUP_W // hd
    ii = lax.broadcasted_iota(jnp.int32, (tb, tb), 0)
    jj = lax.broadcasted_iota(jnp.int32, (tb, tb), 1)
    same = (ii >> 6) == (jj >> 6)
    strict = same & (ii > jj)
    incl = same & (ii >= jj)
    upper = same & (ii <= jj)
    eye = ii == jj
    row8 = lax.broadcasted_iota(jnp.int32, (8, 1), 0)
    gi = lax.broadcasted_iota(jnp.int32, (GROUP_W, GROUP_W), 0)
    gj = lax.broadcasted_iota(jnp.int32, (GROUP_W, GROUP_W), 1)
    head_bd = (gi >> 7) == (gj >> 7)
    ones_bd = jnp.where(head_bd, 1.0, 0.0).astype(BF16)

    def conv_silu(x_ref, idx, cw_ref):
        x = x_ref[...]
        prev8 = halo_ref[idx]
        cw = cw_ref[...]
        acc = x * cw[GDN_CONV - 1:GDN_CONV]
        for s in range(1, GDN_CONV):
            acc = acc + _shift_rows(x, prev8, s, row8) * cw[GDN_CONV - 1 - s:GDN_CONV - s]
        halo_ref[idx] = x[tb - 8:, :]
        return acc * jax.nn.sigmoid(acc)

    def l2n(x):
        return x * lax.rsqrt(_split_dot(x * x, ones_bd, GSUM_PIECES) + 1e-6)

    q = l2n(conv_silu(pq_ref, 0, cq_ref)) * (hd ** -0.5)
    k = l2n(conv_silu(pk_ref, 1, ck_ref))
    v = conv_silu(pv_ref, 2, cv_ref)

    ba = pba_ref[...]
    lane = lax.broadcasted_iota(jnp.int32, (1, 128), 1)
    beta_all = jax.nn.sigmoid(ba)
    xa = ba + gv_ref[1:2, :]
    softplus = jnp.maximum(xa, 0.0) + jnp.log(1.0 + jnp.exp(-jnp.abs(xa)))
    g_all = -jnp.exp(gv_ref[0:1, :]) * softplus

    heads = range(nh)
    qs = [q[:, hh * hd:(hh + 1) * hd] for hh in heads]
    ks = [k[:, hh * hd:(hh + 1) * hd] for hh in heads]
    vs = [v[:, hh * hd:(hh + 1) * hd] for hh in heads]
    k16 = [x.astype(BF16) for x in ks]
    beta = [jnp.sum(jnp.where(lane == grp * nh + hh, beta_all, 0.0), axis=1, keepdims=True) for hh in heads]
    gcol = [jnp.sum(jnp.where(lane == GDN_HEADS + grp * nh + hh, g_all, 0.0), axis=1, keepdims=True)
            for hh in heads]
    gr = [jnp.sum(jnp.where(upper, gcol[hh], 0.0), axis=0, keepdims=True) for hh in heads]
    gc = [jnp.sum(jnp.where(eye, gr[hh], 0.0), axis=1, keepdims=True) for hh in heads]
    dexp = [jnp.exp(jnp.minimum(gc[hh] - gr[hh], 0.0)) for hh in heads]
    tinv = _unit_lower_inverses(
        [jnp.where(strict, beta[hh] * _bdot_nt(k16[hh], k16[hh]) * dexp[hh], 0.0) for hh in heads], ii, jj)
    aqk = [jnp.where(incl, _bdot_nt(qs[hh], k16[hh]) * dexp[hh], 0.0) for hh in heads]
    gamma = [jnp.exp(gc[hh]) for hh in heads]
    wk = [_bdot(tinv[hh], jnp.concatenate([beta[hh] * vs[hh], (beta[hh] * gamma[hh]) * ks[hh]], axis=1))
          for hh in heads]
    aq = [_bdot(aqk[hh], wk[hh]) for hh in heads]
    g_last = [_rows_of_chunk_last(gc[hh], tb) for hh in heads]
    gend = [jnp.exp(g_last[hh]) for hh in heads]
    w0 = jnp.concatenate([wk[hh][:, :hd] for hh in heads], axis=1)
    kw = jnp.concatenate([wk[hh][:, hd:] for hh in heads], axis=1)
    o0 = jnp.concatenate([aq[hh][:, :hd] for hh in heads], axis=1)
    qq = jnp.concatenate([gamma[hh] * qs[hh] - aq[hh][:, hd:] for hh in heads], axis=1)
    kdec = jnp.concatenate([ks[hh] * jnp.exp(g_last[hh] - gc[hh]) for hh in heads], axis=1)
    srow = lax.broadcasted_iota(jnp.int32, (GROUP_W, 1), 0) >> 7

    s = s_ref[...]
    o_chunks = []
    for c in range(tb // CHUNK):
        rs = slice(c * CHUNK, (c + 1) * CHUNK)
        x = _bdot_tn(kdec[rs], jnp.concatenate([kw[rs], w0[rs]], axis=1))
        mcorr = jnp.where(head_bd, x[:, :GROUP_W], 0.0)
        u = jnp.where(head_bd, x[:, GROUP_W:], 0.0)
        o_chunks.append(_bdot(qq[rs], s) + o0[rs])
        decay = gend[0][c * CHUNK:c * CHUNK + 1, :]
        for hh in range(1, nh):
            decay = jnp.where(srow == hh, gend[hh][c * CHUNK:c * CHUNK + 1, :], decay)
        s = decay * s - _bdot(mcorr, s) + u
    s_ref[...] = s

    o = jnp.concatenate(o_chunks, axis=0)
    ms = _split_dot(o * o, ones_bd, GSUM_PIECES) * (1.0 / hd)
    z = pz_ref[...]
    o = o * lax.rsqrt(ms + NORM_EPS) * ng_ref[...] * (z * jax.nn.sigmoid(z))
    o_ref[...] = o.astype(o_ref.dtype)


def _gdn_groups_kernel(pq_ref, pk_ref, pv_ref, pz_ref, pba_ref, cq_ref, ck_ref, cv_ref, gv_ref, ng_ref,
                       o_ref, s_ref, halo_ref, *, tb, ngs):
    grp = pl.program_id(1)

    @pl.when(pl.program_id(2) == 0)
    def _():
        s_ref[...] = jnp.zeros_like(s_ref)
        halo_ref[...] = jnp.zeros_like(halo_ref)

    hd = GDN_HEAD_DIM
    nh = GROUP_W // hd
    groups = range(ngs)
    heads = range(ngs * nh)
    cs = [slice(g * GROUP_W, (g + 1) * GROUP_W) for g in groups]
    ii = lax.broadcasted_iota(jnp.int32, (tb, tb), 0)
    jj = lax.broadcasted_iota(jnp.int32, (tb, tb), 1)
    same = (ii >> 6) == (jj >> 6)
    strict = same & (ii > jj)
    incl = same & (ii >= jj)
    upper = same & (ii <= jj)
    eye = ii == jj
    row8 = lax.broadcasted_iota(jnp.int32, (8, 1), 0)
    gi = lax.broadcasted_iota(jnp.int32, (GROUP_W, GROUP_W), 0)
    gj = lax.broadcasted_iota(jnp.int32, (GROUP_W, GROUP_W), 1)
    head_bd = (gi >> 7) == (gj >> 7)
    ones_bd = jnp.where(head_bd, 1.0, 0.0).astype(BF16)

    def group_sum(x):
        return jnp.concatenate([_split_dot(x[:, c], ones_bd, GSUM_PIECES) for c in cs], axis=1)

    def conv_silu(x_ref, idx, cw_ref):
        x = x_ref[...]
        prev8 = halo_ref[idx]
        cw = cw_ref[...]
        acc = x * cw[GDN_CONV - 1:GDN_CONV]
        for s in range(1, GDN_CONV):
            acc = acc + _shift_rows(x, prev8, s, row8) * cw[GDN_CONV - 1 - s:GDN_CONV - s]
        halo_ref[idx] = x[tb - 8:, :]
        return acc * jax.nn.sigmoid(acc)

    def l2n(x):
        return x * lax.rsqrt(group_sum(x * x) + 1e-6)

    q = l2n(conv_silu(pq_ref, 0, cq_ref)) * (hd ** -0.5)
    k = l2n(conv_silu(pk_ref, 1, ck_ref))
    v = conv_silu(pv_ref, 2, cv_ref)

    ba = pba_ref[...]
    lane = lax.broadcasted_iota(jnp.int32, (1, 128), 1)
    beta_all = jax.nn.sigmoid(ba)
    xa = ba + gv_ref[1:2, :]
    softplus = jnp.maximum(xa, 0.0) + jnp.log(1.0 + jnp.exp(-jnp.abs(xa)))
    g_all = -jnp.exp(gv_ref[0:1, :]) * softplus

    head0 = grp * (ngs * nh)
    qs = [q[:, hh * hd:(hh + 1) * hd] for hh in heads]
    ks = [k[:, hh * hd:(hh + 1) * hd] for hh in heads]
    vs = [v[:, hh * hd:(hh + 1) * hd] for hh in heads]
    k16 = [x.astype(BF16) for x in ks]
    beta = [jnp.sum(jnp.where(lane == head0 + hh, beta_all, 0.0), axis=1, keepdims=True) for hh in heads]
    gcol = [jnp.sum(jnp.where(lane == GDN_HEADS + head0 + hh, g_all, 0.0), axis=1, keepdims=True)
            for hh in heads]
    gr = [jnp.sum(jnp.where(upper, gcol[hh], 0.0), axis=0, keepdims=True) for hh in heads]
    gc = [jnp.sum(jnp.where(eye, gr[hh], 0.0), axis=1, keepdims=True) for hh in heads]
    dexp = [jnp.exp(jnp.minimum(gc[hh] - gr[hh], 0.0)) for hh in heads]
    tinv = _unit_lower_inverses(
        [jnp.where(strict, beta[hh] * _bdot_nt(k16[hh], k16[hh]) * dexp[hh], 0.0) for hh in heads], ii, jj)
    aqk = [jnp.where(incl, _bdot_nt(qs[hh], k16[hh]) * dexp[hh], 0.0) for hh in heads]
    gamma = [jnp.exp(gc[hh]) for hh in heads]
    wk = [_bdot(tinv[hh], jnp.concatenate([beta[hh] * vs[hh], (beta[hh] * gamma[hh]) * ks[hh]], axis=1))
          for hh in heads]
    aq = [_bdot(aqk[hh], wk[hh]) for hh in heads]
    g_last = [_rows_of_chunk_last(gc[hh], tb) for hh in heads]
    gend = [jnp.exp(g_last[hh]) for hh in heads]

    def of_group(g, fn):
        return jnp.concatenate([fn(g * nh + j) for j in range(nh)], axis=1)

    w0 = [of_group(g, lambda hh: wk[hh][:, :hd]) for g in groups]
    kw = [of_group(g, lambda hh: wk[hh][:, hd:]) for g in groups]
    o0 = [of_group(g, lambda hh: aq[hh][:, :hd]) for g in groups]
    qq = [of_group(g, lambda hh: gamma[hh] * qs[hh] - aq[hh][:, hd:]) for g in groups]
    kdec = [of_group(g, lambda hh: ks[hh] * jnp.exp(g_last[hh] - gc[hh])) for g in groups]
    srow = lax.broadcasted_iota(jnp.int32, (GROUP_W, 1), 0) >> 7

    s = [s_ref[g] for g in groups]
    o_chunks = [[] for _ in groups]
    for c in range(tb // CHUNK):
        rs = slice(c * CHUNK, (c + 1) * CHUNK)
        x = [_bdot_tn(kdec[g][rs], jnp.concatenate([kw[g][rs], w0[g][rs]], axis=1)) for g in groups]
        for g in groups:
            o_chunks[g].append(_bdot(qq[g][rs], s[g]) + o0[g][rs])
        new_s = []
        for g in groups:
            decay = gend[g * nh][c * CHUNK:c * CHUNK + 1, :]
            for j in range(1, nh):
                decay = jnp.where(srow == j, gend[g * nh + j][c * CHUNK:c * CHUNK + 1, :], decay)
            mcorr = jnp.where(head_bd, x[g][:, :GROUP_W], 0.0)
            u_add = jnp.where(head_bd, x[g][:, GROUP_W:], 0.0)
            new_s.append(decay * s[g] - _bdot(mcorr, s[g]) + u_add)
        s = new_s
    for g in groups:
        s_ref[g] = s[g]

    o = jnp.concatenate([jnp.concatenate(o_chunks[g], axis=0) for g in groups], axis=1)
    ms = group_sum(o * o) * (1.0 / hd)
    z = pz_ref[...]
    o = o * lax.rsqrt(ms + NORM_EPS) * ng_ref[...] * (z * jax.nn.sigmoid(z))
    o_ref[...] = o.astype(o_ref.dtype)


def gdn_branch(proj, conv_w, a_log, dt_bias, norm_gain, batch, seq_len_p):
    tb = ROW_BLOCK
    ngs = GDN_GROUPS_PER_STEP
    width = ngs * GROUP_W
    nb = seq_len_p // tb
    ng = GDN_W // width
    c0 = C_GDN // width

    def p_spec(off):
        return pl.BlockSpec((tb, width), lambda b, g, i, off=off: (b * nb + i, c0 + off * ng + g))

    def c_spec(off):
        return pl.BlockSpec((GDN_CONV, width), lambda b, g, i, off=off: (0, off * ng + g))

    gvec = jnp.zeros((8, 128), F32)
    gvec = gvec.at[0, GDN_HEADS:2 * GDN_HEADS].set(a_log.astype(F32))
    gvec = gvec.at[1, GDN_HEADS:2 * GDN_HEADS].set(dt_bias.astype(F32))
    ngain = jnp.tile(norm_gain.astype(F32), width // GDN_HEAD_DIM).reshape(1, width)
    cw = conv_w.astype(F32)
    return pl.pallas_call(
        functools.partial(_gdn_groups_kernel, tb=tb, ngs=ngs),
        grid=(batch, ng, nb),
        in_specs=[p_spec(0), p_spec(1), p_spec(2), p_spec(3),
                  pl.BlockSpec((tb, 128), lambda b, g, i: (b * nb + i, C_GDN_BA // 128)),
                  c_spec(0), c_spec(1), c_spec(2),
                  pl.BlockSpec((8, 128), lambda b, g, i: (0, 0)),
                  pl.BlockSpec((1, width), lambda b, g, i: (0, 0))],
        out_specs=pl.BlockSpec((tb, width), lambda b, g, i: (b * nb + i, g)),
        out_shape=jax.ShapeDtypeStruct((batch * seq_len_p, GDN_W), BF16),
        scratch_shapes=[pltpu.VMEM((ngs, GROUP_W, GROUP_W), F32), pltpu.VMEM((3, 8, width), F32)],
        compiler_params=_cparams(("parallel", "parallel", "arbitrary")),
        name="gdn_branch",
    )(proj, proj, proj, proj, proj, cw, cw, cw, gvec, ngain)


def gdn_branch_single(proj, conv_w, a_log, dt_bias, norm_gain, batch, seq_len_p):
    tb = ROW_BLOCK
    nb = seq_len_p // tb
    ng = GDN_W // GROUP_W
    c0 = C_GDN // GROUP_W

    def p_spec(off):
        return pl.BlockSpec((tb, GROUP_W), lambda b, g, i, off=off: (b * nb + i, c0 + off * ng + g))

    def c_spec(off):
        return pl.BlockSpec((GDN_CONV, GROUP_W), lambda b, g, i, off=off: (0, off * ng + g))

    gvec = jnp.zeros((8, 128), F32)
    gvec = gvec.at[0, GDN_HEADS:2 * GDN_HEADS].set(a_log.astype(F32))
    gvec = gvec.at[1, GDN_HEADS:2 * GDN_HEADS].set(dt_bias.astype(F32))
    ngain = jnp.tile(norm_gain.astype(F32), GROUP_W // GDN_HEAD_DIM).reshape(1, GROUP_W)
    cw = conv_w.astype(F32)
    return pl.pallas_call(
        functools.partial(_gdn_kernel, tb=tb),
        grid=(batch, ng, nb),
        in_specs=[p_spec(0), p_spec(1), p_spec(2), p_spec(3),
                  pl.BlockSpec((tb, 128), lambda b, g, i: (b * nb + i, C_GDN_BA // 128)),
                  c_spec(0), c_spec(1), c_spec(2),
                  pl.BlockSpec((8, 128), lambda b, g, i: (0, 0)),
                  pl.BlockSpec((1, GROUP_W), lambda b, g, i: (0, 0))],
        out_specs=pl.BlockSpec((tb, GROUP_W), lambda b, g, i: (b * nb + i, g)),
        out_shape=jax.ShapeDtypeStruct((batch * seq_len_p, GDN_W), BF16),
        scratch_shapes=[pltpu.VMEM((GROUP_W, GROUP_W), F32), pltpu.VMEM((3, 8, GROUP_W), F32)],
        compiler_params=_cparams(("parallel", "parallel", "arbitrary")),
        name="gdn_branch",
    )(proj, proj, proj, proj, proj, cw, cw, cw, gvec, ngain)


def _rwkv_kernel(pr_ref, pk_ref, pv_ref, plo_ref, mur_ref, muk_ref, muv_ref, mul_ref, vec_ref,
                 wup_ref, aup_ref, gup_ref, o_ref, s_ref, prev_ref, prevl_ref, *, tb):
    @pl.when(pl.program_id(2) == 0)
    def _():
        s_ref[...] = jnp.zeros_like(s_ref)
        prev_ref[...] = jnp.zeros_like(prev_ref)
        prevl_ref[...] = jnp.zeros_like(prevl_ref)

    hd = RW_HEAD_DIM
    nh = GROUP_W // hd
    ii = lax.broadcasted_iota(jnp.int32, (tb, tb), 0)
    jj = lax.broadcasted_iota(jnp.int32, (tb, tb), 1)
    same = (ii >> 6) == (jj >> 6)
    strict = same & (ii > jj)
    incl = same & (ii >= jj)
    tri = jnp.where(incl, 1.0, 0.0).astype(BF16)
    gi = lax.broadcasted_iota(jnp.int32, (GROUP_W, GROUP_W), 0)
    gj = lax.broadcasted_iota(jnp.int32, (GROUP_W, GROUP_W), 1)
    head_bd = (gi >> 6) == (gj >> 6)
    ones_bd = jnp.where(head_bd, 1.0, 0.0).astype(BF16)
    lane_head = lax.broadcasted_iota(jnp.int32, (1, GROUP_W), 1) >> 6
    row0 = lax.broadcasted_iota(jnp.int32, (tb, 1), 0) == 0

    def lerp_shift(x_ref, prev, mu_ref):
        x = x_ref[...]
        xs = jnp.where(row0, prev[7:8, :], pltpu.roll(x, 1, 0))
        return x + (xs - x) * mu_ref[...], x[tb - 8:, :]

    r, prev_r = lerp_shift(pr_ref, prev_ref[0], mur_ref)
    kx, prev_k = lerp_shift(pk_ref, prev_ref[1], muk_ref)
    v, prev_v = lerp_shift(pv_ref, prev_ref[2], muv_ref)
    lo, prev_l = lerp_shift(plo_ref, prevl_ref[...], mul_ref)
    prev_ref[0] = prev_r
    prev_ref[1] = prev_k
    prev_ref[2] = prev_v
    prevl_ref[...] = prev_l

    w0, a0, k_k, k_a = vec_ref[0:1, :], vec_ref[1:2, :], vec_ref[2:3, :], vec_ref[3:4, :]
    r_k, gn_g, gn_b = vec_ref[4:5, :], vec_ref[5:6, :], vec_ref[6:7, :]
    zdec = w0 + _bdot(jnp.tanh(lo[:, :RW_DECAY_LORA]), wup_ref[...])
    lw = jax.nn.sigmoid(zdec) * (-math.exp(-0.5))
    a = jax.nn.sigmoid(a0 + _bdot(lo[:, RW_DECAY_LORA:RW_DECAY_LORA + RW_A_LORA], aup_ref[...]))
    gate = _bdot(jax.nn.sigmoid(lo[:, RW_DECAY_LORA + RW_A_LORA:]), gup_ref[...])
    kkx = kx * k_k
    kk = kkx * lax.rsqrt(_split_dot(kkx * kkx, ones_bd, GSUM_PIECES) + 1e-6)
    k2 = kx * (1.0 + (a - 1.0) * k_a)
    b = kk * a

    lwc = _split_dot_left(tri, lw, 3)
    w_inv = jnp.exp(-lwc)
    rt = r * jnp.exp(lwc)
    kt = k2 * w_inv
    bt = b * w_inv
    kap = kk * jnp.exp(lwc - lw)
    l_last = _rows_of_chunk_last(lwc, tb)
    dec = jnp.exp(l_last - lwc)
    kw_end = k2 * dec
    bw_end = b * dec
    w_end = jnp.exp(l_last)

    heads = range(nh)
    kt16, bt16 = kt.astype(BF16), bt.astype(BF16)
    kap_m = [jnp.where(lane_head == h, kap, 0.0).astype(BF16) for h in heads]
    rt_m = [jnp.where(lane_head == h, rt, 0.0).astype(BF16) for h in heads]
    v_m = [jnp.where(lane_head == h, v, 0.0).astype(BF16) for h in heads]
    tinv = _unit_lower_inverses([jnp.where(strict, _bdot_nt(kap_m[h], bt16), 0.0) for h in heads], ii, jj)
    tinv = [t.astype(BF16) for t in tinv]
    akk_v = [_bdot(jnp.where(strict, _bdot_nt(kap_m[h], kt16), 0.0), v_m[h]) for h in heads]
    a_rk = [jnp.where(incl, _bdot_nt(rt_m[h], kt16), 0.0).astype(BF16) for h in heads]
    a_rb = [jnp.where(incl, _bdot_nt(rt_m[h], bt16), 0.0).astype(BF16) for h in heads]
    nxt = [lane_head == (h + 1) % nh for h in heads]
    own = [lane_head == h for h in heads]
    kp_p0 = [_bdot(tinv[h], kap_m[h] + pltpu.roll(akk_v[h], hd, 1).astype(BF16)) for h in heads]
    rb = [_bdot(a_rb[h], kp_p0[h]) for h in heads]
    ark_v = [_bdot(a_rk[h], v_m[h]) for h in heads]
    zero = jnp.zeros((tb, GROUP_W), F32)
    kp = sum([jnp.where(own[h], kp_p0[h], 0.0) for h in heads], zero)
    p0 = pltpu.roll(sum([jnp.where(nxt[h], kp_p0[h], 0.0) for h in heads], zero), GROUP_W - hd, 1)
    rq = rt - sum([jnp.where(own[h], rb[h], 0.0) for h in heads], zero)
    o0 = sum(ark_v, zero) - pltpu.roll(sum([jnp.where(nxt[h], rb[h], 0.0) for h in heads], zero),
                                       GROUP_W - hd, 1)

    s = s_ref[...]
    o_chunks = []
    for c in range(tb // CHUNK):
        rs = slice(c * CHUNK, (c + 1) * CHUNK)
        mcorr = jnp.where(head_bd, _bdot_tn(kp[rs], bw_end[rs]), 0.0)
        u = jnp.where(head_bd, _bdot_tn(jnp.concatenate([v[rs], p0[rs]], axis=0),
                                        jnp.concatenate([kw_end[rs], -bw_end[rs]], axis=0)), 0.0)
        o_chunks.append(_bdot_nt(rq[rs], s) + o0[rs])
        s = s * w_end[c * CHUNK:c * CHUNK + 1, :] - _bdot(s, mcorr) + u
    s_ref[...] = s

    o = jnp.concatenate(o_chunks, axis=0)
    mean = _split_dot(o, ones_bd, GSUM_PIECES) * (1.0 / hd)
    d = o - mean
    var = _split_dot(d * d, ones_bd, GSUM_PIECES) * (1.0 / hd)
    o = d * lax.rsqrt(var + RW_GN_EPS) * gn_g + gn_b
    o = o + _split_dot(r * k2 * r_k, ones_bd, GSUM_PIECES) * v
    o_ref[...] = (o * gate).astype(o_ref.dtype)


def rwkv_branch_single(proj, mu, w0, w_up, a0, a_up, g_up, k_k, k_a, r_k, gn_gain, gn_bias, batch, seq_len_p):
    tb = ROW_BLOCK
    nb = seq_len_p // tb
    ng = RW_W // GROUP_W
    c0 = C_RW // GROUP_W
    n_lora = RW_DECAY_LORA + RW_A_LORA + RW_GATE_LORA

    def p_spec(off):
        return pl.BlockSpec((tb, GROUP_W), lambda b, g, i, off=off: (b * nb + i, c0 + off * ng + g))

    def mu_spec(off):
        return pl.BlockSpec((1, GROUP_W), lambda b, g, i, off=off: (0, off * ng + g))

    def w_spec(rows):
        return pl.BlockSpec((rows, GROUP_W), lambda b, g, i: (0, g))

    mu = mu.astype(F32)
    mu_rkv = mu[:3 * RW_W].reshape(1, 3 * RW_W)
    mu_l = jnp.pad(mu[3 * RW_W:], (0, RW_LORA_P - n_lora)).reshape(1, RW_LORA_P)
    vec = jnp.stack([w0, a0, k_k, k_a, r_k, gn_gain, gn_bias, jnp.zeros_like(w0)]).astype(F32)
    g_up_p = jnp.pad(g_up, ((0, RW_LORA_P - RW_DECAY_LORA - RW_A_LORA - RW_GATE_LORA), (0, 0))).astype(BF16)
    return pl.pallas_call(
        functools.partial(_rwkv_kernel, tb=tb),
        grid=(batch, ng, nb),
        in_specs=[p_spec(0), p_spec(1), p_spec(2),
                  pl.BlockSpec((tb, RW_LORA_P), lambda b, g, i: (b * nb + i, C_RW_LORA // RW_LORA_P)),
                  mu_spec(0), mu_spec(1), mu_spec(2),
                  pl.BlockSpec((1, RW_LORA_P), lambda b, g, i: (0, 0)),
                  w_spec(8), w_spec(RW_DECAY_LORA), w_spec(RW_A_LORA),
                  w_spec(RW_LORA_P - RW_DECAY_LORA - RW_A_LORA)],
        out_specs=pl.BlockSpec((tb, GROUP_W), lambda b, g, i: (b * nb + i, g)),
        out_shape=jax.ShapeDtypeStruct((batch * seq_len_p, RW_W), BF16),
        scratch_shapes=[pltpu.VMEM((GROUP_W, GROUP_W), F32), pltpu.VMEM((3, 8, GROUP_W), F32),
                        pltpu.VMEM((8, RW_LORA_P), F32)],
        compiler_params=_cparams(("parallel", "parallel", "arbitrary")),
        name="rwkv_branch",
    )(proj, proj, proj, proj, mu_rkv, mu_rkv, mu_rkv, mu_l, vec,
      w_up.astype(BF16), a_up.astype(BF16), g_up_p)


def _rwkv_groups_kernel(pr_ref, pk_ref, pv_ref, plo_ref, mur_ref, muk_ref, muv_ref, mul_ref, vec_ref,
                        wup_ref, aup_ref, gup_ref, o_ref, s_ref, prev_ref, prevl_ref, *, tb, ngs):
    @pl.when(pl.program_id(2) == 0)
    def _():
        s_ref[...] = jnp.zeros_like(s_ref)
        prev_ref[...] = jnp.zeros_like(prev_ref)
        prevl_ref[...] = jnp.zeros_like(prevl_ref)

    hd = RW_HEAD_DIM
    nh = GROUP_W // hd
    groups = range(ngs)
    cs = [slice(g * GROUP_W, (g + 1) * GROUP_W) for g in groups]
    units = [(g, h) for g in groups for h in range(nh)]
    ii = lax.broadcasted_iota(jnp.int32, (tb, tb), 0)
    jj = lax.broadcasted_iota(jnp.int32, (tb, tb), 1)
    same = (ii >> 6) == (jj >> 6)
    strict = same & (ii > jj)
    incl = same & (ii >= jj)
    tri = jnp.where(incl, 1.0, 0.0).astype(BF16)
    gi = lax.broadcasted_iota(jnp.int32, (GROUP_W, GROUP_W), 0)
    gj = lax.broadcasted_iota(jnp.int32, (GROUP_W, GROUP_W), 1)
    head_bd = (gi >> 6) == (gj >> 6)
    ones_bd = jnp.where(head_bd, 1.0, 0.0).astype(BF16)
    lane_head = lax.broadcasted_iota(jnp.int32, (1, GROUP_W), 1) >> 6
    row0 = lax.broadcasted_iota(jnp.int32, (tb, 1), 0) == 0

    def lerp_shift(x_ref, prev, mu_ref):
        x = x_ref[...]
        xs = jnp.where(row0, prev[7:8, :], pltpu.roll(x, 1, 0))
        return x + (xs - x) * mu_ref[...], x[tb - 8:, :]

    def group_sum(x):
        return jnp.concatenate([_split_dot(x[:, c], ones_bd, GSUM_PIECES) for c in cs], axis=1)

    r, prev_r = lerp_shift(pr_ref, prev_ref[0], mur_ref)
    kx, prev_k = lerp_shift(pk_ref, prev_ref[1], muk_ref)
    v, prev_v = lerp_shift(pv_ref, prev_ref[2], muv_ref)
    lo, prev_l = lerp_shift(plo_ref, prevl_ref[...], mul_ref)
    prev_ref[0] = prev_r
    prev_ref[1] = prev_k
    prev_ref[2] = prev_v
    prevl_ref[...] = prev_l

    w0, a0, k_k, k_a = vec_ref[0:1, :], vec_ref[1:2, :], vec_ref[2:3, :], vec_ref[3:4, :]
    r_k, gn_g, gn_b = vec_ref[4:5, :], vec_ref[5:6, :], vec_ref[6:7, :]
    zdec = w0 + _bdot(jnp.tanh(lo[:, :RW_DECAY_LORA]), wup_ref[...])
    lw = jax.nn.sigmoid(zdec) * (-math.exp(-0.5))
    a = jax.nn.sigmoid(a0 + _bdot(lo[:, RW_DECAY_LORA:RW_DECAY_LORA + RW_A_LORA], aup_ref[...]))
    gate = _bdot(jax.nn.sigmoid(lo[:, RW_DECAY_LORA + RW_A_LORA:]), gup_ref[...])
    kkx = kx * k_k
    kk = kkx * lax.rsqrt(group_sum(kkx * kkx) + 1e-6)
    k2 = kx * (1.0 + (a - 1.0) * k_a)
    b = kk * a

    lwc = _split_dot_left(tri, lw, 3)
    w_inv = jnp.exp(-lwc)
    rt = r * jnp.exp(lwc)
    kt = k2 * w_inv
    bt = b * w_inv
    kap = kk * jnp.exp(lwc - lw)
    l_last = _rows_of_chunk_last(lwc, tb)
    dec = jnp.exp(l_last - lwc)
    kw_end = k2 * dec
    bw_end = b * dec
    w_end = jnp.exp(l_last)

    kt16 = [kt[:, c].astype(BF16) for c in cs]
    bt16 = [bt[:, c].astype(BF16) for c in cs]
    kap_m = [jnp.where(lane_head == h, kap[:, cs[g]], 0.0).astype(BF16) for g, h in units]
    rt_m = [jnp.where(lane_head == h, rt[:, cs[g]], 0.0).astype(BF16) for g, h in units]
    v_m = [jnp.where(lane_head == h, v[:, cs[g]], 0.0).astype(BF16) for g, h in units]
    nu = range(len(units))
    tinv = _unit_lower_inverses(
        [jnp.where(strict, _bdot_nt(kap_m[u], bt16[units[u][0]]), 0.0) for u in nu], ii, jj)
    tinv = [t.astype(BF16) for t in tinv]
    akk_v = [_bdot(jnp.where(strict, _bdot_nt(kap_m[u], kt16[units[u][0]]), 0.0), v_m[u]) for u in nu]
    a_rk = [jnp.where(incl, _bdot_nt(rt_m[u], kt16[units[u][0]]), 0.0).astype(BF16) for u in nu]
    a_rb = [jnp.where(incl, _bdot_nt(rt_m[u], bt16[units[u][0]]), 0.0).astype(BF16) for u in nu]
    kp_p0 = [_bdot(tinv[u], kap_m[u] + pltpu.roll(akk_v[u], hd, 1).astype(BF16)) for u in nu]
    rb = [_bdot(a_rb[u], kp_p0[u]) for u in nu]
    ark_v = [_bdot(a_rk[u], v_m[u]) for u in nu]
    zero = jnp.zeros((tb, GROUP_W), F32)

    def own_sum(xs, g):
        return sum([jnp.where(lane_head == h, xs[u], 0.0) for u, (gg, h) in enumerate(units) if gg == g], zero)

    def next_sum(xs, g):
        t = sum([jnp.where(lane_head == (h + 1) % nh, xs[u], 0.0)
                 for u, (gg, h) in enumerate(units) if gg == g], zero)
        return pltpu.roll(t, GROUP_W - hd, 1)

    kp = [own_sum(kp_p0, g) for g in groups]
    p0 = [next_sum(kp_p0, g) for g in groups]
    rq = [rt[:, cs[g]] - own_sum(rb, g) for g in groups]
    o0 = [sum([ark_v[u] for u, (gg, h) in enumerate(units) if gg == g], zero) - next_sum(rb, g)
          for g in groups]

    s = [s_ref[g] for g in groups]
    o_chunks = [[] for _ in groups]
    for c in range(tb // CHUNK):
        rs = slice(c * CHUNK, (c + 1) * CHUNK)
        mcorr = [jnp.where(head_bd, _bdot_tn(kp[g][rs], bw_end[rs, cs[g]]), 0.0) for g in groups]
        u_add = [jnp.where(head_bd, _bdot_tn(jnp.concatenate([v[rs, cs[g]], p0[g][rs]], axis=0),
                                             jnp.concatenate([kw_end[rs, cs[g]], -bw_end[rs, cs[g]]], axis=0)),
                           0.0) for g in groups]
        for g in groups:
            o_chunks[g].append(_bdot_nt(rq[g][rs], s[g]) + o0[g][rs])
        s = [s[g] * w_end[c * CHUNK:c * CHUNK + 1, cs[g]] - _bdot(s[g], mcorr[g]) + u_add[g] for g in groups]
    for g in groups:
        s_ref[g] = s[g]

    o = jnp.concatenate([jnp.concatenate(o_chunks[g], axis=0) for g in groups], axis=1)
    mean = group_sum(o) * (1.0 / hd)
    d = o - mean
    var = group_sum(d * d) * (1.0 / hd)
    o = d * lax.rsqrt(var + RW_GN_EPS) * gn_g + gn_b
    o = o + group_sum(r * k2 * r_k) * v
    o_ref[...] = (o * gate).astype(o_ref.dtype)


def rwkv_branch(proj, mu, w0, w_up, a0, a_up, g_up, k_k, k_a, r_k, gn_gain, gn_bias, batch, seq_len_p):
    tb = ROW_BLOCK
    ngs = RW_GROUPS_PER_STEP
    width = ngs * GROUP_W
    nb = seq_len_p // tb
    ng = RW_W // width
    c0 = C_RW // width
    n_lora = RW_DECAY_LORA + RW_A_LORA + RW_GATE_LORA

    def p_spec(off):
        return pl.BlockSpec((tb, width), lambda b, g, i, off=off: (b * nb + i, c0 + off * ng + g))

    def mu_spec(off):
        return pl.BlockSpec((1, width), lambda b, g, i, off=off: (0, off * ng + g))

    def w_spec(rows):
        return pl.BlockSpec((rows, width), lambda b, g, i: (0, g))

    mu = mu.astype(F32)
    mu_rkv = mu[:3 * RW_W].reshape(1, 3 * RW_W)
    mu_l = jnp.pad(mu[3 * RW_W:], (0, RW_LORA_P - n_lora)).reshape(1, RW_LORA_P)
    vec = jnp.stack([w0, a0, k_k, k_a, r_k, gn_gain, gn_bias, jnp.zeros_like(w0)]).astype(F32)
    g_up_p = jnp.pad(g_up, ((0, RW_LORA_P - n_lora), (0, 0))).astype(BF16)
    return pl.pallas_call(
        functools.partial(_rwkv_groups_kernel, tb=tb, ngs=ngs),
        grid=(batch, ng, nb),
        in_specs=[p_spec(0), p_spec(1), p_spec(2),
                  pl.BlockSpec((tb, RW_LORA_P), lambda b, g, i: (b * nb + i, C_RW_LORA // RW_LORA_P)),
                  mu_spec(0), mu_spec(1), mu_spec(2),
                  pl.BlockSpec((1, RW_LORA_P), lambda b, g, i: (0, 0)),
                  w_spec(8), w_spec(RW_DECAY_LORA), w_spec(RW_A_LORA),
                  w_spec(RW_LORA_P - RW_DECAY_LORA - RW_A_LORA)],
        out_specs=pl.BlockSpec((tb, width), lambda b, g, i: (b * nb + i, g)),
        out_shape=jax.ShapeDtypeStruct((batch * seq_len_p, RW_W), BF16),
        scratch_shapes=[pltpu.VMEM((ngs, GROUP_W, GROUP_W), F32), pltpu.VMEM((3, 8, width), F32),
                        pltpu.VMEM((8, RW_LORA_P), F32)],
        compiler_params=_cparams(("parallel", "parallel", "arbitrary")),
        name="rwkv_branch",
    )(proj, proj, proj, proj, mu_rkv, mu_rkv, mu_rkv, mu_l, vec,
      w_up.astype(BF16), a_up.astype(BF16), g_up_p)


def _pad_in_weights(w_in_i):
    d = w_in_i.shape[0]
    gdn_cols = 4 * GDN_W + 2 * GDN_HEADS
    rw_cols = 3 * RW_W + RW_DECAY_LORA + RW_A_LORA + RW_GATE_LORA
    o1 = 3072
    o2 = o1 + gdn_cols
    o3 = o2 + rw_cols

    def z(n):
        return jnp.zeros((d, n), w_in_i.dtype)

    parts = [w_in_i[:, :o1],
             w_in_i[:, o1:o1 + 4 * GDN_W],
             w_in_i[:, o2:o2 + 3 * RW_W],
             w_in_i[:, o3:],
             w_in_i[:, o1 + 4 * GDN_W:o2], z(128 - 2 * GDN_HEADS),
             w_in_i[:, o2 + 3 * RW_W:o3], z(RW_LORA_P - (rw_cols - 3 * RW_W))]
    return jnp.concatenate(parts, axis=1).astype(BF16)


def kernel(x, meta_tokens, pre_mix_gain, post_mix_gain, pre_mlp_gain, post_mlp_gain, w_in, b_gate, da_lambda, da_norm_gain, gdn_conv, gdn_a_log, gdn_dt_bias, gdn_norm_gain, rw_mu, rw_w0, rw_w_up, rw_a0, rw_a_up, rw_g_up, rw_k_k, rw_k_a, rw_r_k, rw_gn_gain, rw_gn_bias, w_da_out, w_gdn_out, w_rw_out, w_out, w_ff1, w_ff2):
    batch, seq, d_model = x.shape
    depth = w_in.shape[0]
    n_tok = N_META + seq
    lo, hi = PAD_F, PAD_F + n_tok
    lp = -(-hi // 1280) * 1280 if hi > 1280 else -(-hi // ROW_BLOCK) * ROW_BLOCK
    m = batch * lp

    meta = jnp.broadcast_to(meta_tokens[None].astype(x.dtype), (batch, N_META, d_model))
    h = jnp.concatenate([jnp.zeros((batch, PAD_F, d_model), x.dtype), meta, x,
                         jnp.zeros((batch, lp - hi, d_model), x.dtype)], axis=1).reshape(m, d_model)

    inv = 1.0 / (ROPE_THETA ** (jnp.arange(0, DA_HEAD_DIM, 2, dtype=F32) / DA_HEAD_DIM))
    ang = (jnp.arange(lp, dtype=F32) - PAD_F)[:, None] * inv[None, :]
    cos_t = jnp.concatenate([jnp.cos(ang), jnp.cos(ang)], axis=1)
    sin_t = jnp.concatenate([-jnp.sin(ang), jnp.sin(ang)], axis=1)

    xn = rmsnorm_cast(h, pre_mix_gain[0])
    for i in range(depth):
        lam_init = 0.8 - 0.6 * math.exp(-0.3 * i)
        proj = matmul(xn, _pad_in_weights(w_in[i]))

        q, k, v = rope_prep(proj, cos_t, sin_t, lp)
        o_da = diff_attention(q, k, v, da_lambda[i], da_norm_gain[i], batch, lp, lam_init)
        o_gdn = gdn_branch(proj, gdn_conv[i], gdn_a_log[i], gdn_dt_bias[i], gdn_norm_gain[i], batch, lp)
        o_rw = rwkv_branch(proj, rw_mu[i], rw_w0[i], rw_w_up[i], rw_a0[i], rw_a_up[i], rw_g_up[i],
                           rw_k_k[i], rw_k_a[i], rw_r_k[i], rw_gn_gain[i], rw_gn_bias[i], batch, lp)

        mixed = branch_mix(o_da, o_gdn, o_rw, w_da_out[i].astype(BF16), w_gdn_out[i].astype(BF16),
                           w_rw_out[i].astype(BF16), proj, b_gate[i], d_model)
        h, xn = mm_norm_residual(mixed, w_out[i].astype(BF16), h, post_mix_gain[i], pre_mlp_gain[i], lp, lo, hi)
        hid = matmul(xn, w_ff1[i].astype(BF16), out_dtype=BF16, relu2=True)
        next_gain = pre_mix_gain[i + 1] if i + 1 < depth else pre_mix_gain[i]
        h, xn = mm_norm_residual(hid, w_ff2[i].astype(BF16), h, post_mlp_gain[i], next_gain, lp, lo, hi)

    return h.reshape(batch, lp, d_model)[:, hi - seq:hi]
```
